```python
import math
import jax, jax.numpy as jnp
from jax import lax
import numpy as np

D_MODEL = 4096
BATCH = 4
SEQ = 2048
DEPTH = 1

SGU_WIDTH = D_MODEL
SGU_GROUPS = 16
SGU_GROUP_DIM = SGU_WIDTH // SGU_GROUPS
CHUNK = 128
DIFF_HEADS = 16
DIFF_HEAD_DIM = D_MODEL // (2 * DIFF_HEADS)
DIFF_V_DIM = 2 * DIFF_HEAD_DIM
DIFF_QK_WIDTH = DIFF_HEADS * 2 * DIFF_HEAD_DIM
DIFF_WIDTH = DIFF_HEADS * DIFF_V_DIM
Q_BLOCK = 128
D_FF = 11008
CONV_WIDTH = 3
EPS = 1e-6
NEG_INF = -1e30
IN_SPLITS = (SGU_WIDTH, SGU_WIDTH, DIFF_QK_WIDTH, DIFF_QK_WIDTH, DIFF_WIDTH, SGU_WIDTH, DIFF_WIDTH)
IN_WIDTH = sum(IN_SPLITS)

kernel_name = "hybrid_sgu_diffattn_convffn_adaln"


def _rms(x, g):
    xf = x.astype(jnp.float32)
    y = xf * lax.rsqrt(jnp.mean(xf * xf, axis=-1, keepdims=True) + EPS)
    return (y * g.astype(jnp.float32)).astype(x.dtype)


def _lambda_init(layer_idx):
    return 0.8 - 0.6 * math.exp(-0.3 * layer_idx)


def _spatial_gating(u, v, sgu_norm_g, w_spatial, b_spatial):
    B, S, _ = u.shape
    v = _rms(v, sgu_norm_g)
    vc = v.reshape(B, S // CHUNK, CHUNK, SGU_GROUPS, SGU_GROUP_DIM)
    causal = jnp.tril(jnp.ones((CHUNK, CHUNK), dtype=bool))
    ws = jnp.where(causal[None], w_spatial, jnp.zeros_like(w_spatial))
    mixed = jnp.einsum('gts,bcsge->bctge', ws, vc) + b_spatial.T[:, :, None]
    return u * mixed.reshape(B, S, SGU_WIDTH)


def _diff_attention(q, k, v, q_norm_g, k_norm_g, lq1, lk1, lq2, lk2, subln_g, lambda_init):
    B, S, _ = q.shape
    H, d = DIFF_HEADS, DIFF_HEAD_DIM
    q = _rms(q.reshape(B, S, H, 2, d), q_norm_g)
    k = _rms(k.reshape(B, S, H, 2, d), k_norm_g)
    v = v.reshape(B, S, H, DIFF_V_DIM)
    f32 = jnp.float32
    lam = (jnp.exp(jnp.sum(lq1.astype(f32) * lk1.astype(f32)))
           - jnp.exp(jnp.sum(lq2.astype(f32) * lk2.astype(f32))) + lambda_init)
    slopes = 2.0 ** (-8.0 * jnp.arange(1, H + 1, dtype=f32) / H)
    nb = S // Q_BLOCK
    qb = q.reshape(B, nb, Q_BLOCK, H, 2, d).transpose(1, 0, 2, 3, 4, 5)
    k_pos = jnp.arange(S)
    scale = d ** -0.5

    def block(args):
        qi, i = args
        q_pos = i * Q_BLOCK + jnp.arange(Q_BLOCK)
        s = jnp.einsum('bqhcd,bkhcd->bhcqk', qi, k, preferred_element_type=f32) * scale
        dist = (q_pos[:, None] - k_pos[None, :]).astype(f32)
        s = s - slopes[:, None, None, None] * dist
        s = jnp.where(dist >= 0, s, NEG_INF)
        p = jax.nn.softmax(s, axis=-1)
        a = p[:, :, 0] - lam * p[:, :, 1]
        return jnp.einsum('bhqk,bkhe->bqhe', a.astype(v.dtype), v)

    o = lax.map(block, (qb, jnp.arange(nb)))
    o = o.transpose(1, 0, 2, 3, 4).reshape(B, S, H, DIFF_V_DIM)
    o = _rms(o, subln_g) * (1.0 - lambda_init)
    return o.reshape(B, S, DIFF_WIDTH)


def _conv_ffn(h, w_up, conv_w, conv_b, w_down):
    a = h @ w_up
    C = a.shape[-1]
    a = lax.conv_general_dilated(
        a, conv_w[:, None, :].astype(a.dtype), window_strides=(1,),
        padding=[(CONV_WIDTH - 1, 0)], dimension_numbers=('NWC', 'WIO', 'NWC'),
        feature_group_count=C) + conv_b
    gate, val = jnp.split(a, 2, axis=-1)
    return (jax.nn.silu(gate) * val) @ w_down


def setup_inputs(seed: int = 0) -> dict:
    key = jax.random.key(seed)
    ks = jax.random.split(key, 24)
    L, D, F = DEPTH, D_MODEL, D_FF
    nrm = lambda k, shape, s: jax.random.normal(k, shape, jnp.float32) * s
    return {
        "x": nrm(ks[0], (BATCH, SEQ, D), 1.0),
        "c": nrm(ks[1], (BATCH, D), 1.0),
        "w_ada": nrm(ks[2], (L, D, 6 * D), 0.5 * D ** -0.5),
        "b_ada": nrm(ks[3], (L, 6 * D), 0.02),
        "norm1_g": 1.0 + nrm(ks[4], (L, D), 0.02),
        "norm2_g": 1.0 + nrm(ks[5], (L, D), 0.02),
        "w_in": nrm(ks[6], (L, D, IN_WIDTH), D ** -0.5),
        "sgu_norm_g": 1.0 + nrm(ks[7], (L, SGU_WIDTH), 0.02),
        "w_spatial": nrm(ks[8], (L, SGU_GROUPS, CHUNK, CHUNK), CHUNK ** -0.5),
        "b_spatial": 1.0 + nrm(ks[9], (L, SGU_GROUPS, CHUNK), 0.02),
        "q_norm_g": 1.0 + nrm(ks[10], (L, DIFF_HEAD_DIM), 0.02),
        "k_norm_g": 1.0 + nrm(ks[11], (L, DIFF_HEAD_DIM), 0.02),
        "lambda_q1": nrm(ks[12], (L, DIFF_HEAD_DIM), 0.1),
        "lambda_k1": nrm(ks[13], (L, DIFF_HEAD_DIM), 0.1),
        "lambda_q2": nrm(ks[14], (L, DIFF_HEAD_DIM), 0.1),
        "lambda_k2": nrm(ks[15], (L, DIFF_HEAD_DIM), 0.1),
        "subln_g": 1.0 + nrm(ks[16], (L, DIFF_V_DIM), 0.02),
        "w_out": nrm(ks[17], (L, D, D), D ** -0.5),
        "w_ff_up": nrm(ks[18], (L, D, 2 * F), D ** -0.5),
        "conv_w": nrm(ks[19], (L, CONV_WIDTH, 2 * F), CONV_WIDTH ** -0.5),
        "conv_b": nrm(ks[20], (L, 2 * F), 0.02),
        "w_ff_down": nrm(ks[21], (L, F, D), F ** -0.5),
    }


def reference(x, c, w_ada, b_ada, norm1_g, norm2_g, w_in, sgu_norm_g, w_spatial, b_spatial,
              q_norm_g, k_norm_g, lambda_q1, lambda_k1, lambda_q2, lambda_k2, subln_g,
              w_out, w_ff_up, conv_w, conv_b, w_ff_down):
    split_idx = [int(i) for i in np.cumsum(IN_SPLITS)[:-1]]
    c_act = jax.nn.silu(c)
    for l in range(DEPTH):
        lambda_init = _lambda_init(l)
        mod = c_act @ w_ada[l] + b_ada[l]
        shift1, scale1, gate1, shift2, scale2, gate2 = [m[:, None, :] for m in jnp.split(mod, 6, axis=-1)]
        h = _rms(x, norm1_g[l]) * (1.0 + scale1) + shift1
        z = h @ w_in[l]
        zu, zv, zq, zk, zvb, za, zb = jnp.split(z, split_idx, axis=-1)
        y_a = _spatial_gating(jax.nn.gelu(zu, approximate=False), jax.nn.gelu(zv, approximate=False),
                              sgu_norm_g[l], w_spatial[l], b_spatial[l])
        y_b = _diff_attention(zq, zk, zvb, q_norm_g[l], k_norm_g[l], lambda_q1[l], lambda_k1[l],
                              lambda_q2[l], lambda_k2[l], subln_g[l], lambda_init)
        y = jax.nn.sigmoid(za) * y_a + jax.nn.sigmoid(zb) * y_b
        x = x + gate1 * (y @ w_out[l])
        h2 = _rms(x, norm2_g[l]) * (1.0 + scale2) + shift2
        x = x + gate2 * _conv_ffn(h2, w_ff_up[l], conv_w[l], conv_b[l], w_ff_down[l])
    return x
```

```python
import functools
import math

import jax
import jax.numpy as jnp
from jax import lax
from jax.experimental import pallas as pl
from jax.experimental.pallas import tpu as pltpu

F32 = jnp.float32
BF16 = jnp.bfloat16

SGU_GROUPS = 16
CHUNK = 128
DIFF_HEADS = 16
CONV_WIDTH = 3
EPS = 1e-6
NEG_INF = -1e30

LANES = 128
VMEM_LIMIT_BYTES = 60000 * 1024
ADA_ROWS = 16


def _params(semantics):
    return pltpu.CompilerParams(dimension_semantics=semantics, vmem_limit_bytes=VMEM_LIMIT_BYTES)


def _gelu(x):
    return 0.5 * x * (1.0 + lax.erf(x * (1.0 / math.sqrt(2.0))))


def _ada_kernel(c_ref, w_ref, b_ref, o_ref):
    c = c_ref[...]
    c_act = (c * jax.nn.sigmoid(c)).astype(BF16)
    acc = jnp.dot(c_act, w_ref[...].astype(BF16), preferred_element_type=F32)
    o_ref[...] = acc + b_ref[...]


def _ada(c_pad, w_ada, b_ada, tn=512):
    rows, d = c_pad.shape
    n = w_ada.shape[1]
    return pl.pallas_call(
        _ada_kernel,
        grid=(n // tn,),
        in_specs=[pl.BlockSpec((rows, d), lambda j: (0, 0)),
                  pl.BlockSpec((d, tn), lambda j: (0, j)),
                  pl.BlockSpec((1, tn), lambda j: (0, j))],
        out_specs=pl.BlockSpec((rows, tn), lambda j: (0, j)),
        out_shape=jax.ShapeDtypeStruct((rows, n), F32),
        compiler_params=_params(("arbitrary",)),
        name="ada",
    )(c_pad, w_ada, b_ada)


def _norm_mod_kernel(x_ref, g_ref, scale_ref, shift_ref, o_ref):
    x = x_ref[0]
    ms = jnp.mean(x * x, axis=-1, keepdims=True)
    y = x * lax.rsqrt(ms + EPS) * g_ref[...]
    o_ref[0] = (y * (1.0 + scale_ref[0]) + shift_ref[0]).astype(BF16)


def _norm_mod(x, g, mod, scale_idx, shift_idx, tr=256):
    b, s, d = x.shape
    return pl.pallas_call(
        _norm_mod_kernel,
        grid=(b, s // tr),
        in_specs=[pl.BlockSpec((1, tr, d), lambda i, r: (i, r, 0)),
                  pl.BlockSpec((1, d), lambda i, r: (0, 0)),
                  pl.BlockSpec((1, 1, d), lambda i, r: (i * 6 + scale_idx, 0, 0)),
                  pl.BlockSpec((1, 1, d), lambda i, r: (i * 6 + shift_idx, 0, 0))],
        out_specs=pl.BlockSpec((1, tr, d), lambda i, r: (i, r, 0)),
        out_shape=jax.ShapeDtypeStruct((b, s, d), BF16),
        compiler_params=_params(("arbitrary", "arbitrary")),
        name="norm_mod",
    )(x, g, mod, mod)


def _proj_kernel(h_ref, *refs, n_w, tw, epilogue):
    w_refs, rest, wbf_ref = refs[:n_w], refs[n_w:-1], refs[-1]
    for s, w_ref in enumerate(w_refs):
        wbf_ref[:, s * tw:(s + 1) * tw] = w_ref[...].astype(BF16)
    acc = jnp.dot(h_ref[0], wbf_ref[...], preferred_element_type=F32)
    epilogue(acc, *rest)


def _proj(h, w, w_offsets, tw, n_tiles, epilogue, extra, extra_specs, out_shape, out_specs, name):
    r, tm, k = h.shape
    n_w = len(w_offsets)
    h_spec = pl.BlockSpec((1, tm, k), lambda i, j: (i, 0, 0), pipeline_mode=pl.Buffered(1))
    w_specs = [pl.BlockSpec((k, tw), functools.partial(lambda i, j, off: (0, off + j), off=off))
               for off in w_offsets]
    return pl.pallas_call(
        functools.partial(_proj_kernel, n_w=n_w, tw=tw, epilogue=epilogue),
        grid=(r, n_tiles),
        in_specs=[h_spec] + w_specs + list(extra_specs),
        out_specs=out_specs,
        out_shape=out_shape,
        scratch_shapes=[pltpu.VMEM((k, n_w * tw), BF16)],
        compiler_params=_params(("arbitrary", "arbitrary")),
        name=name,
    )(h, *([w] * n_w), *extra)


def _ug_epilogue(acc, o_ref):
    tw = acc.shape[1] // 2
    o_ref[0] = _gelu(acc[:, :tw]) * jax.nn.sigmoid(acc[:, tw:])


def _v_epilogue(acc, o_ref, ssq_ref):
    g = _gelu(acc)
    o_ref[0] = g

    @pl.when(pl.program_id(1) == 0)
    def _():
        ssq_ref[...] = jnp.zeros_like(ssq_ref)

    sq = g * g
    part = sq[:, :LANES]
    for k in range(1, acc.shape[1] // LANES):
        part = part + sq[:, k * LANES:(k + 1) * LANES]
    ssq_ref[0] += part


def _qk_epilogue(acc, g_ref, o_ref):
    for k in range(acc.shape[1] // LANES):
        z = acc[:, k * LANES:(k + 1) * LANES]
        ms = jnp.mean(z * z, axis=-1, keepdims=True)
        o_ref[0, :, k * LANES:(k + 1) * LANES] = (
            z * lax.rsqrt(ms + EPS) * g_ref[:, k * LANES:(k + 1) * LANES]).astype(BF16)


def _cast_epilogue(acc, o_ref):
    o_ref[0] = acc.astype(BF16)


def _sigmoid_epilogue(acc, o_ref):
    o_ref[0] = jax.nn.sigmoid(acc)


def _residual_epilogue(acc, x_ref, gate_ref, o_ref):
    o_ref[0] = x_ref[0] + gate_ref[0] * acc


def _conv_swiglu_epilogue(acc, cwg_ref, cwv_ref, cbg_ref, cbv_ref, o_ref):
    tw = acc.shape[1] // 2
    row = lax.broadcasted_iota(jnp.int32, acc.shape, 0)
    a1 = jnp.where(row >= 1, pltpu.roll(acc, 1, 0), 0.0)
    a2 = jnp.where(row >= 2, pltpu.roll(acc, 2, 0), 0.0)

    def conv(lo, cw_ref, cb_ref):
        sl = slice(lo, lo + tw)
        return (cw_ref[0:1, :] * a2[:, sl] + cw_ref[1:2, :] * a1[:, sl]
                + cw_ref[2:3, :] * acc[:, sl] + cb_ref[...])

    gate = conv(0, cwg_ref, cbg_ref)
    val = conv(tw, cwv_ref, cbv_ref)
    o_ref[0] = (gate * jax.nn.sigmoid(gate) * val).astype(BF16)


def _sgu_kernel(gv_ref, ssq_ref, ug_ref, g_ref, ws_ref, bt_ref, o_ref):
    d = gv_ref.shape[2]
    gd = d // SGU_GROUPS
    ms = jnp.sum(ssq_ref[0], axis=-1, keepdims=True) * (1.0 / d)
    r = lax.rsqrt(ms + EPS)
    row = lax.broadcasted_iota(jnp.int32, (CHUNK, CHUNK), 0)
    col = lax.broadcasted_iota(jnp.int32, (CHUNK, CHUNK), 1)
    causal = row >= col
    for g in range(SGU_GROUPS):
        sl = slice(g * gd, (g + 1) * gd)
        vn = (gv_ref[0, :, sl] * r * g_ref[:, sl]).astype(BF16)
        ws = jnp.where(causal, ws_ref[g], 0.0).astype(BF16)
        mixed = jnp.dot(ws, vn, preferred_element_type=F32) + bt_ref[:, g:g + 1]
        o_ref[0, :, sl] = ug_ref[0, :, sl] * mixed


def _sgu(gv, ssq, ug, sgu_g, w_spatial, b_spatial_t):
    b, s, d = gv.shape
    row_spec = pl.BlockSpec((1, CHUNK, d), lambda i, c: (i, c, 0))
    return pl.pallas_call(
        _sgu_kernel,
        grid=(b, s // CHUNK),
        in_specs=[row_spec,
                  pl.BlockSpec((1, CHUNK, LANES), lambda i, c: (i, c, 0)),
                  row_spec,
                  pl.BlockSpec((1, d), lambda i, c: (0, 0)),
                  pl.BlockSpec((SGU_GROUPS, CHUNK, CHUNK), lambda i, c: (0, 0, 0)),
                  pl.BlockSpec((CHUNK, SGU_GROUPS), lambda i, c: (0, 0))],
        out_specs=row_spec,
        out_shape=jax.ShapeDtypeStruct((b, s, d), F32),
        compiler_params=_params(("arbitrary", "arbitrary")),
        name="sgu",
    )(gv, ssq, ug, sgu_g, w_spatial, b_spatial_t)


def _attn_kernel(q_ref, k_ref, v_ref, ya_ref, sgb_ref, lam_ref, subg_ref, o_ref,
                 acc_ref, m_ref, l_ref, *, tq, hd, lambda_init):
    head = pl.program_id(1)
    qi = pl.program_id(2)
    scale = hd ** -0.5
    slope = jnp.exp2(jnp.full((1, 1), -8.0 / DIFF_HEADS, F32) * (head + 1).astype(F32))
    row = lax.broadcasted_iota(jnp.int32, (tq, tq), 0)
    col = lax.broadcasted_iota(jnp.int32, (tq, tq), 1)
    rel_bias = slope * (row - col).astype(F32)
    on_or_below_diag = row >= col

    m_ref[...] = jnp.full_like(m_ref, NEG_INF)
    l_ref[...] = jnp.zeros_like(l_ref)
    acc_ref[...] = jnp.zeros_like(acc_ref)

    def block(j, masked):
        start = pl.multiple_of(j * tq, tq)
        kj = k_ref[0, pl.ds(start, tq), :]
        vj = v_ref[0, pl.ds(start, tq), :]
        bias = rel_bias + slope * ((qi - j) * tq).astype(F32)
        for c in range(2):
            qc = q_ref[0, :, c * hd:(c + 1) * hd]
            kc = kj[:, c * hd:(c + 1) * hd]
            s = lax.dot_general(qc, kc, (((1,), (1,)), ((), ())), preferred_element_type=F32)
            s = s * scale - bias
            if masked:
                s = jnp.where(on_or_below_diag, s, NEG_INF)
            m_prev = m_ref[c]
            m_new = jnp.maximum(m_prev, jnp.max(s, axis=-1, keepdims=True))
            alpha = jnp.exp(m_prev - m_new)
            p = jnp.exp(s - m_new)
            l_ref[c] = alpha * l_ref[c] + jnp.sum(p, axis=-1, keepdims=True)
            acc_ref[c] = alpha * acc_ref[c] + jnp.dot(p.astype(BF16), vj,
                                                      preferred_element_type=F32)
            m_ref[c] = m_new

    def body(j, carry):
        block(j, False)
        return carry

    lax.fori_loop(0, qi, body, 0)
    block(qi, True)

    lv = lam_ref[...]
    lam = (jnp.exp(jnp.sum(lv[0:1] * lv[1:2], axis=-1, keepdims=True))
           - jnp.exp(jnp.sum(lv[2:3] * lv[3:4], axis=-1, keepdims=True)) + lambda_init)
    o = acc_ref[0] / l_ref[0] - lam * (acc_ref[1] / l_ref[1])
    ms = jnp.mean(o * o, axis=-1, keepdims=True)
    yb = o * lax.rsqrt(ms + EPS) * subg_ref[...] * (1.0 - lambda_init)
    o_ref[0] = (sgb_ref[0] * yb + ya_ref[0]).astype(BF16)


def _attn(qk, v, ya, sgb, lam_rows, subln_g, lambda_init, tq=256):
    b, s, d = v.shape
    vd = d // DIFF_HEADS
    hd = vd // 2
    q_spec = pl.BlockSpec((1, tq, vd), lambda i, h, t: (i, t, h))
    k_spec = pl.BlockSpec((1, s, vd), lambda i, h, t: (i, 0, DIFF_HEADS + h))
    v_spec = pl.BlockSpec((1, s, vd), lambda i, h, t: (i, 0, h))
    return pl.pallas_call(
        functools.partial(_attn_kernel, tq=tq, hd=hd, lambda_init=lambda_init),
        grid=(b, DIFF_HEADS, s // tq),
        in_specs=[q_spec, k_spec, v_spec, q_spec, q_spec,
                  pl.BlockSpec((4, hd), lambda i, h, t: (0, 0)),
                  pl.BlockSpec((1, vd), lambda i, h, t: (0, 0))],
        out_specs=q_spec,
        out_shape=jax.ShapeDtypeStruct((b, s, d), BF16),
        scratch_shapes=[pltpu.VMEM((2, tq, vd), F32),
                        pltpu.VMEM((2, tq, 1), F32),
                        pltpu.VMEM((2, tq, 1), F32)],
        compiler_params=_params(("arbitrary", "arbitrary", "arbitrary")),
        name="attn",
    )(qk, qk, v, ya, sgb, lam_rows, subln_g)


def _down_kernel(g_ref, w_ref, x_ref, gate_ref, o_ref):
    acc = jnp.dot(g_ref[...], w_ref[...], preferred_element_type=F32)
    o_ref[...] = x_ref[...] + gate_ref[0] * acc


def _down(g, w_bf16, x1, mod, gate_idx, seq, tm=512, tn=512):
    m, f = g.shape
    n = w_bf16.shape[1]
    tiles_per_seq = seq // tm
    return pl.pallas_call(
        _down_kernel,
        grid=(m // tm, n // tn),
        in_specs=[pl.BlockSpec((tm, f), lambda i, j: (i, 0)),
                  pl.BlockSpec((f, tn), lambda i, j: (0, j)),
                  pl.BlockSpec((tm, tn), lambda i, j: (i, j)),
                  pl.BlockSpec((1, 1, tn), lambda i, j: ((i // tiles_per_seq) * 6 + gate_idx, 0, j))],
        out_specs=pl.BlockSpec((tm, tn), lambda i, j: (i, j)),
        out_shape=jax.ShapeDtypeStruct((m, n), F32),
        compiler_params=_params(("arbitrary", "arbitrary")),
        name="down",
    )(g, w_bf16, x1, mod)


MXU_TILE = 256
PAIR_TW = MXU_TILE
WIDE_TW = 2 * MXU_TILE


def _layer(x, c_pad, lambda_init, w_ada, b_ada, norm1_g, norm2_g, w_in, sgu_norm_g, w_spatial,
           b_spatial, q_norm_g, k_norm_g, lam_rows, subln_g, w_out, w_ff_up, conv_w, conv_b,
           w_ff_down):
    b, s, d = x.shape
    f = w_ff_down.shape[0]
    mod = _ada(c_pad, w_ada, b_ada)[:b].reshape(b * 6, 1, d)

    def out3(dtype, n=d):
        return jax.ShapeDtypeStruct((b, s, n), dtype)

    def tile_spec(width):
        return pl.BlockSpec((1, s, width), lambda i, j: (i, 0, j))

    h = _norm_mod(x, norm1_g, mod, scale_idx=1, shift_idx=0)

    pair_seg, wide_seg = d // PAIR_TW, d // WIDE_TW
    ug = _proj(h, w_in, (0 * pair_seg, 5 * pair_seg), PAIR_TW, pair_seg, _ug_epilogue, (), (),
               out3(F32), tile_spec(PAIR_TW), "proj_ug")
    gv, ssq = _proj(h, w_in, (1 * wide_seg,), WIDE_TW, wide_seg, _v_epilogue, (), (),
                    (out3(F32), out3(F32, LANES)),
                    (tile_spec(WIDE_TW), pl.BlockSpec((1, s, LANES), lambda i, j: (i, 0, 0))),
                    "proj_v")
    hd = q_norm_g.shape[-1]
    qk_g = jnp.concatenate([jnp.tile(q_norm_g, d // hd), jnp.tile(k_norm_g, d // hd)])[None]
    qk = _proj(h, w_in, (2 * wide_seg,), WIDE_TW, 2 * wide_seg, _qk_epilogue, (qk_g,),
               (pl.BlockSpec((1, WIDE_TW), lambda i, j: (0, j)),),
               out3(BF16, 2 * d), tile_spec(WIDE_TW), "proj_qk")
    vb = _proj(h, w_in, (4 * wide_seg,), WIDE_TW, wide_seg, _cast_epilogue, (), (),
               out3(BF16), tile_spec(WIDE_TW), "proj_vb")
    sgb = _proj(h, w_in, (6 * wide_seg,), WIDE_TW, wide_seg, _sigmoid_epilogue, (), (),
                out3(F32), tile_spec(WIDE_TW), "proj_gb")

    ya = _sgu(gv, ssq, ug, sgu_norm_g[None], w_spatial, b_spatial.T)
    y = _attn(qk, vb, ya, sgb, lam_rows, subln_g[None], lambda_init)

    tm = s // 2
    rows3 = (b * s // tm, tm, d)
    x1 = _proj(y.reshape(rows3), w_out, (0,), WIDE_TW, d // WIDE_TW, _residual_epilogue,
               (x.reshape(rows3), mod),
               (pl.BlockSpec((1, tm, WIDE_TW), lambda i, j: (i, 0, j)),
                pl.BlockSpec((1, 1, WIDE_TW), lambda i, j: ((i // (s // tm)) * 6 + 2, 0, j))),
               jax.ShapeDtypeStruct(rows3, F32),
               pl.BlockSpec((1, tm, WIDE_TW), lambda i, j: (i, 0, j)), "proj_out")
    x1 = x1.reshape(b, s, d)

    h2 = _norm_mod(x1, norm2_g, mod, scale_idx=4, shift_idx=3)
    val_off = f // PAIR_TW
    conv_b2 = conv_b[None]
    g = _proj(h2, w_ff_up, (0, val_off), PAIR_TW, val_off, _conv_swiglu_epilogue,
              (conv_w, conv_w, conv_b2, conv_b2),
              (pl.BlockSpec((CONV_WIDTH, PAIR_TW), lambda i, j: (0, j)),
               pl.BlockSpec((CONV_WIDTH, PAIR_TW), lambda i, j: (0, val_off + j)),
               pl.BlockSpec((1, PAIR_TW), lambda i, j: (0, j)),
               pl.BlockSpec((1, PAIR_TW), lambda i, j: (0, val_off + j))),
              out3(BF16, f), tile_spec(PAIR_TW), "proj_up")

    out = _down(g.reshape(b * s, f), w_ff_down.astype(BF16), x1.reshape(b * s, d), mod,
                gate_idx=5, seq=s)
    return out.reshape(b, s, d)


def kernel(x, c, w_ada, b_ada, norm1_g, norm2_g, w_in, sgu_norm_g, w_spatial, b_spatial, q_norm_g, k_norm_g, lambda_q1, lambda_k1, lambda_q2, lambda_k2, subln_g, w_out, w_ff_up, conv_w, conv_b, w_ff_down):
    depth = w_in.shape[0]
    c_pad = jnp.pad(c, ((0, ADA_ROWS - c.shape[0]), (0, 0)))
    for l in range(depth):
        lam_rows = jnp.stack([lambda_q1[l], lambda_k1[l], lambda_q2[l], lambda_k2[l]])
        x = _layer(x, c_pad, 0.8 - 0.6 * math.exp(-0.3 * l), w_ada[l], b_ada[l][None],
                   norm1_g[l][None], norm2_g[l][None], w_in[l], sgu_norm_g[l], w_spatial[l],
                   b_spatial[l], q_norm_g[l], k_norm_g[l], lam_rows, subln_g[l], w_out[l],
                   w_ff_up[l], conv_w[l], conv_b[l], w_ff_down[l])
    return x
```

```python
import functools
import math

import jax
import jax.numpy as jnp
from jax import lax
from jax.experimental import pallas as pl
from jax.experimental.pallas import tpu as pltpu

F32 = jnp.float32
BF16 = jnp.bfloat16

SGU_GROUPS = 16
CHUNK = 128
DIFF_HEADS = 16
CONV_WIDTH = 3
EPS = 1e-6
NEG_INF = -1e30

LANES = 128
VMEM_LIMIT_BYTES = 60000 * 1024
ADA_ROWS = 16


def _params(semantics):
    return pltpu.CompilerParams(dimension_semantics=semantics, vmem_limit_bytes=VMEM_LIMIT_BYTES)


def _gelu(x):
    return 0.5 * x * (1.0 + lax.erf(x * (1.0 / math.sqrt(2.0))))


def _ada_kernel(c_ref, w_ref, b_ref, o_ref):
    c = c_ref[...]
    c_act = (c * jax.nn.sigmoid(c)).astype(BF16)
    acc = jnp.dot(c_act, w_ref[...].astype(BF16), preferred_element_type=F32)
    o_ref[...] = acc + b_ref[...]


def _ada(c_pad, w_ada, b_ada, tn=512):
    rows, d = c_pad.shape
    n = w_ada.shape[1]
    return pl.pallas_call(
        _ada_kernel,
        grid=(n // tn,),
        in_specs=[pl.BlockSpec((rows, d), lambda j: (0, 0)),
                  pl.BlockSpec((d, tn), lambda j: (0, j)),
                  pl.BlockSpec((1, tn), lambda j: (0, j))],
        out_specs=pl.BlockSpec((rows, tn), lambda j: (0, j)),
        out_shape=jax.ShapeDtypeStruct((rows, n), F32),
        compiler_params=_params(("arbitrary",)),
        name="ada",
    )(c_pad, w_ada, b_ada)


def _norm_mod_kernel(x_ref, g_ref, scale_ref, shift_ref, o_ref):
    x = x_ref[0]
    ms = jnp.mean(x * x, axis=-1, keepdims=True)
    y = x * lax.rsqrt(ms + EPS) * g_ref[...]
    o_ref[0] = (y * (1.0 + scale_ref[0]) + shift_ref[0]).astype(BF16)


def _norm_mod(x, g, mod, scale_idx, shift_idx, tr=256):
    b, s, d = x.shape
    return pl.pallas_call(
        _norm_mod_kernel,
        grid=(b, s // tr),
        in_specs=[pl.BlockSpec((1, tr, d), lambda i, r: (i, r, 0)),
                  pl.BlockSpec((1, d), lambda i, r: (0, 0)),
                  pl.BlockSpec((1, 1, d), lambda i, r: (i * 6 + scale_idx, 0, 0)),
                  pl.BlockSpec((1, 1, d), lambda i, r: (i * 6 + shift_idx, 0, 0))],
        out_specs=pl.BlockSpec((1, tr, d), lambda i, r: (i, r, 0)),
        out_shape=jax.ShapeDtypeStruct((b, s, d), BF16),
        compiler_params=_params(("arbitrary", "arbitrary")),
        name="norm_mod",
    )(x, g, mod, mod)


def _proj_kernel(h_ref, *refs, n_w, tw, epilogue):
    w_refs, rest, wbf_ref = refs[:n_w], refs[n_w:-1], refs[-1]
    for s, w_ref in enumerate(w_refs):
        wbf_ref[:, s * tw:(s + 1) * tw] = w_ref[...].astype(BF16)
    acc = jnp.dot(h_ref[0], wbf_ref[...], preferred_element_type=F32)
    epilogue(acc, *rest)


def _proj(h, w, w_offsets, tw, n_tiles, epilogue, extra, extra_specs, out_shape, out_specs, name):
    r, tm, k = h.shape
    n_w = len(w_offsets)
    h_spec = pl.BlockSpec((1, tm, k), lambda i, j: (i, 0, 0), pipeline_mode=pl.Buffered(1))
    w_specs = [pl.BlockSpec((k, tw), functools.partial(lambda i, j, off: (0, off + j), off=off))
               for off in w_offsets]
    return pl.pallas_call(
        functools.partial(_proj_kernel, n_w=n_w, tw=tw, epilogue=epilogue),
        grid=(r, n_tiles),
        in_specs=[h_spec] + w_specs + list(extra_specs),
        out_specs=out_specs,
        out_shape=out_shape,
        scratch_shapes=[pltpu.VMEM((k, n_w * tw), BF16)],
        compiler_params=_params(("arbitrary", "arbitrary")),
        name=name,
    )(h, *([w] * n_w), *extra)


def _ug_epilogue(acc, o_ref):
    tw = acc.shape[1] // 2
    o_ref[0] = _gelu(acc[:, :tw]) * jax.nn.sigmoid(acc[:, tw:])


def _v_epilogue(acc, o_ref, ssq_ref):
    g = _gelu(acc)
    o_ref[0] = g

    @pl.when(pl.program_id(1) == 0)
    def _():
        ssq_ref[...] = jnp.zeros_like(ssq_ref)

    sq = g * g
    part = sq[:, :LANES]
    for k in range(1, acc.shape[1] // LANES):
        part = part + sq[:, k * LANES:(k + 1) * LANES]
    ssq_ref[0] += part


def _qk_epilogue(acc, g_ref, o_ref):
    for k in range(acc.shape[1] // LANES):
        z = acc[:, k * LANES:(k + 1) * LANES]
        ms = jnp.mean(z * z, axis=-1, keepdims=True)
        o_ref[0, :, k * LANES:(k + 1) * LANES] = (
            z * lax.rsqrt(ms + EPS) * g_ref[:, k * LANES:(k + 1) * LANES]).astype(BF16)


def _transpose_epilogue(acc, o_ref):
    o_ref[0] = acc.T.astype(BF16)


def _sigmoid_epilogue(acc, o_ref):
    o_ref[0] = jax.nn.sigmoid(acc)


def _residual_epilogue(acc, x_ref, gate_ref, o_ref):
    o_ref[0] = x_ref[0] + gate_ref[0] * acc


def _conv_swiglu_epilogue(acc, cwg_ref, cwv_ref, cbg_ref, cbv_ref, o_ref):
    tw = acc.shape[1] // 2
    row = lax.broadcasted_iota(jnp.int32, acc.shape, 0)
    a1 = jnp.where(row >= 1, pltpu.roll(acc, 1, 0), 0.0)
    a2 = jnp.where(row >= 2, pltpu.roll(acc, 2, 0), 0.0)

    def conv(lo, cw_ref, cb_ref):
        sl = slice(lo, lo + tw)
        return (cw_ref[0:1, :] * a2[:, sl] + cw_ref[1:2, :] * a1[:, sl]
                + cw_ref[2:3, :] * acc[:, sl] + cb_ref[...])

    gate = conv(0, cwg_ref, cbg_ref)
    val = conv(tw, cwv_ref, cbv_ref)
    o_ref[0] = (gate * jax.nn.sigmoid(gate) * val).astype(BF16)


def _sgu_kernel(gv_ref, ssq_ref, ug_ref, g_ref, ws_ref, bt_ref, o_ref):
    d = gv_ref.shape[2]
    gd = d // SGU_GROUPS
    ms = jnp.sum(ssq_ref[0], axis=-1, keepdims=True) * (1.0 / d)
    r = lax.rsqrt(ms + EPS)
    row = lax.broadcasted_iota(jnp.int32, (CHUNK, CHUNK), 0)
    col = lax.broadcasted_iota(jnp.int32, (CHUNK, CHUNK), 1)
    causal = row >= col
    for g in range(SGU_GROUPS):
        sl = slice(g * gd, (g + 1) * gd)
        vn = (gv_ref[0, :, sl] * r * g_ref[:, sl]).astype(BF16)
        ws = jnp.where(causal, ws_ref[g], 0.0).astype(BF16)
        mixed = jnp.dot(ws, vn, preferred_element_type=F32) + bt_ref[:, g:g + 1]
        o_ref[0, :, sl] = ug_ref[0, :, sl] * mixed


def _sgu(gv, ssq, ug, sgu_g, w_spatial, b_spatial_t):
    b, s, d = gv.shape
    row_spec = pl.BlockSpec((1, CHUNK, d), lambda i, c: (i, c, 0))
    return pl.pallas_call(
        _sgu_kernel,
        grid=(b, s // CHUNK),
        in_specs=[row_spec,
                  pl.BlockSpec((1, CHUNK, LANES), lambda i, c: (i, c, 0)),
                  row_spec,
                  pl.BlockSpec((1, d), lambda i, c: (0, 0)),
                  pl.BlockSpec((SGU_GROUPS, CHUNK, CHUNK), lambda i, c: (0, 0, 0)),
                  pl.BlockSpec((CHUNK, SGU_GROUPS), lambda i, c: (0, 0))],
        out_specs=row_spec,
        out_shape=jax.ShapeDtypeStruct((b, s, d), F32),
        compiler_params=_params(("arbitrary", "arbitrary")),
        name="sgu",
    )(gv, ssq, ug, sgu_g, w_spatial, b_spatial_t)


def _attn_kernel(q_ref, k_ref, vt_ref, ya_ref, sgb_ref, lam_ref, subg_ref, o_ref,
                 *, tq, hd, lambda_init):
    s_len = q_ref.shape[1]
    head = pl.program_id(1)
    log2e = math.log2(math.e)
    c1 = hd ** -0.5 * log2e
    slope2 = log2e * jnp.exp2(jnp.full((1, 1), -8.0 / DIFF_HEADS, F32) * (head + 1).astype(F32))
    kv_i = lax.broadcasted_iota(jnp.int32, (tq, tq), 0)
    q_i = lax.broadcasted_iota(jnp.int32, (tq, tq), 1)
    rel_bias = slope2 * (q_i - kv_i).astype(F32)
    causal = q_i >= kv_i

    lv = lam_ref[...]
    lam = (jnp.exp(jnp.sum(lv[0:1] * lv[1:2], axis=-1, keepdims=True))
           - jnp.exp(jnp.sum(lv[2:3] * lv[3:4], axis=-1, keepdims=True)) + lambda_init)

    for qi in range(s_len // tq):
        kv_len = (qi + 1) * tq
        q_rows = slice(qi * tq, (qi + 1) * tq)
        chunk_off = [slope2 * float(tq * (qi - r)) for r in range(qi + 1)]
        probs, norms = [], []
        for c in range(2):
            qc = q_ref[0, q_rows, c * hd:(c + 1) * hd]
            kc = k_ref[0, :kv_len, c * hd:(c + 1) * hd]
            s_all = lax.dot_general(kc, qc, (((1,), (1,)), ((), ())),
                                    preferred_element_type=F32)
            t, m = [], None
            for r in range(qi + 1):
                tr = s_all[r * tq:(r + 1) * tq] * c1 - rel_bias
                if r == qi:
                    tr = jnp.where(causal, tr, NEG_INF)
                mr = jnp.max(tr, axis=0, keepdims=True) - chunk_off[r]
                m = mr if m is None else jnp.maximum(m, mr)
                t.append(tr)
            p, l = [], None
            for r in range(qi + 1):
                pr = jnp.exp2(t[r] - (m + chunk_off[r]))
                lr = jnp.sum(pr, axis=0, keepdims=True)
                l = lr if l is None else l + lr
                p.append(pr)
            probs.append(p)
            norms.append(l)
        w0 = 1.0 / norms[0]
        w1 = lam / norms[1]
        a = jnp.concatenate([(probs[0][r] * w0 - probs[1][r] * w1).astype(BF16)
                             for r in range(qi + 1)], axis=0)
        o = jnp.dot(vt_ref[0, :, :kv_len], a, preferred_element_type=F32).T
        ms = jnp.mean(o * o, axis=-1, keepdims=True)
        yb = o * lax.rsqrt(ms + EPS) * subg_ref[...] * (1.0 - lambda_init)
        o_ref[0, q_rows, :] = (sgb_ref[0, q_rows, :] * yb + ya_ref[0, q_rows, :]).astype(BF16)


def _attn(qk, vt, ya, sgb, lam_rows, subln_g, lambda_init, tq=256):
    b, d, s = vt.shape
    vd = d // DIFF_HEADS
    hd = vd // 2
    head_spec = pl.BlockSpec((1, s, vd), lambda i, h: (i, 0, h))
    return pl.pallas_call(
        functools.partial(_attn_kernel, tq=tq, hd=hd, lambda_init=lambda_init),
        grid=(b, DIFF_HEADS),
        in_specs=[head_spec,
                  pl.BlockSpec((1, s, vd), lambda i, h: (i, 0, DIFF_HEADS + h)),
                  pl.BlockSpec((1, vd, s), lambda i, h: (i, h, 0)),
                  head_spec, head_spec,
                  pl.BlockSpec((4, hd), lambda i, h: (0, 0)),
                  pl.BlockSpec((1, vd), lambda i, h: (0, 0))],
        out_specs=head_spec,
        out_shape=jax.ShapeDtypeStruct((b, s, d), BF16),
        compiler_params=_params(("arbitrary", "arbitrary")),
        name="attn",
    )(qk, qk, vt, ya, sgb, lam_rows, subln_g)


def _down_kernel(g_ref, w_ref, x_ref, gate_ref, o_ref):
    acc = jnp.dot(g_ref[...], w_ref[...], preferred_element_type=F32)
    o_ref[...] = x_ref[...] + gate_ref[0] * acc


def _down(g, w_bf16, x1, mod, gate_idx, seq, tm=512, tn=512):
    m, f = g.shape
    n = w_bf16.shape[1]
    tiles_per_seq = seq // tm
    return pl.pallas_call(
        _down_kernel,
        grid=(m // tm, n // tn),
        in_specs=[pl.BlockSpec((tm, f), lambda i, j: (i, 0)),
                  pl.BlockSpec((f, tn), lambda i, j: (0, j)),
                  pl.BlockSpec((tm, tn), lambda i, j: (i, j)),
                  pl.BlockSpec((1, 1, tn), lambda i, j: ((i // tiles_per_seq) * 6 + gate_idx, 0, j))],
        out_specs=pl.BlockSpec((tm, tn), lambda i, j: (i, j)),
        out_shape=jax.ShapeDtypeStruct((m, n), F32),
        compiler_params=_params(("arbitrary", "arbitrary")),
        name="down",
    )(g, w_bf16, x1, mod)


MXU_TILE = 256
PAIR_TW = MXU_TILE
WIDE_TW = 2 * MXU_TILE


def _layer(x, c_pad, lambda_init, w_ada, b_ada, norm1_g, norm2_g, w_in, sgu_norm_g, w_spatial,
           b_spatial, q_norm_g, k_norm_g, lam_rows, subln_g, w_out, w_ff_up, conv_w, conv_b,
           w_ff_down):
    b, s, d = x.shape
    f = w_ff_down.shape[0]
    mod = _ada(c_pad, w_ada, b_ada)[:b].reshape(b * 6, 1, d)

    def out3(dtype, n=d):
        return jax.ShapeDtypeStruct((b, s, n), dtype)

    def tile_spec(width):
        return pl.BlockSpec((1, s, width), lambda i, j: (i, 0, j))

    h = _norm_mod(x, norm1_g, mod, scale_idx=1, shift_idx=0)

    pair_seg, wide_seg = d // PAIR_TW, d // WIDE_TW
    ug = _proj(h, w_in, (0 * pair_seg, 5 * pair_seg), PAIR_TW, pair_seg, _ug_epilogue, (), (),
               out3(F32), tile_spec(PAIR_TW), "proj_ug")
    gv, ssq = _proj(h, w_in, (1 * wide_seg,), WIDE_TW, wide_seg, _v_epilogue, (), (),
                    (out3(F32), out3(F32, LANES)),
                    (tile_spec(WIDE_TW), pl.BlockSpec((1, s, LANES), lambda i, j: (i, 0, 0))),
                    "proj_v")
    hd = q_norm_g.shape[-1]
    qk_g = jnp.concatenate([jnp.tile(q_norm_g, d // hd), jnp.tile(k_norm_g, d // hd)])[None]
    qk = _proj(h, w_in, (2 * wide_seg,), WIDE_TW, 2 * wide_seg, _qk_epilogue, (qk_g,),
               (pl.BlockSpec((1, WIDE_TW), lambda i, j: (0, j)),),
               out3(BF16, 2 * d), tile_spec(WIDE_TW), "proj_qk")
    vbt = _proj(h, w_in, (4 * wide_seg,), WIDE_TW, wide_seg, _transpose_epilogue, (), (),
                jax.ShapeDtypeStruct((b, d, s), BF16),
                pl.BlockSpec((1, WIDE_TW, s), lambda i, j: (i, j, 0)), "proj_vb")
    sgb = _proj(h, w_in, (6 * wide_seg,), WIDE_TW, wide_seg, _sigmoid_epilogue, (), (),
                out3(F32), tile_spec(WIDE_TW), "proj_gb")

    ya = _sgu(gv, ssq, ug, sgu_norm_g[None], w_spatial, b_spatial.T)
    y = _attn(qk, vbt, ya, sgb, lam_rows, subln_g[None], lambda_init)

    tm = s // 2
    rows3 = (b * s // tm, tm, d)
    x1 = _proj(y.reshape(rows3), w_out, (0,), WIDE_TW, d // WIDE_TW, _residual_epilogue,
               (x.reshape(rows3), mod),
               (pl.BlockSpec((1, tm, WIDE_TW), lambda i, j: (i, 0, j)),
                pl.BlockSpec((1, 1, WIDE_TW), lambda i, j: ((i // (s // tm)) * 6 + 2, 0, j))),
               jax.ShapeDtypeStruct(rows3, F32),
               pl.BlockSpec((1, tm, WIDE_TW), lambda i, j: (i, 0, j)), "proj_out")
    x1 = x1.reshape(b, s, d)

    h2 = _norm_mod(x1, norm2_g, mod, scale_idx=4, shift_idx=3)
    val_off = f // PAIR_TW
    conv_b2 = conv_b[None]
    g = _proj(h2, w_ff_up, (0, val_off), PAIR_TW, val_off, _conv_swiglu_epilogue,
              (conv_w, conv_w, conv_b2, conv_b2),
              (pl.BlockSpec((CONV_WIDTH, PAIR_TW), lambda i, j: (0, j)),
               pl.BlockSpec((CONV_WIDTH, PAIR_TW), lambda i, j: (0, val_off + j)),
               pl.BlockSpec((1, PAIR_TW), lambda i, j: (0, j)),
               pl.BlockSpec((1, PAIR_TW), lambda i, j: (0, val_off + j))),
              out3(BF16, f), tile_spec(PAIR_TW), "proj_up")

    out = _down(g.reshape(b * s, f), w_ff_down.astype(BF16), x1.reshape(b * s, d), mod,
                gate_idx=5, seq=s)
    return out.reshape(b, s, d)


def kernel(x, c, w_ada, b_ada, norm1_g, norm2_g, w_in, sgu_norm_g, w_spatial, b_spatial, q_norm_g, k_norm_g, lambda_q1, lambda_k1, lambda_q2, lambda_k2, subln_g, w_out, w_ff_up, conv_w, conv_b, w_ff_down):
    depth = w_in.shape[0]
    c_pad = jnp.pad(c, ((0, ADA_ROWS - c.shape[0]), (0, 0)))
    for l in range(depth):
        lam_rows = jnp.stack([lambda_q1[l], lambda_k1[l], lambda_q2[l], lambda_k2[l]])
        x = _layer(x, c_pad, 0.8 - 0.6 * math.exp(-0.3 * l), w_ada[l], b_ada[l][None],
                   norm1_g[l][None], norm2_g[l][None], w_in[l], sgu_norm_g[l], w_spatial[l],
                   b_spatial[l], q_norm_g[l], k_norm_g[l], lam_rows, subln_g[l], w_out[l],
                   w_ff_up[l], conv_w[l], conv_b[l], w_ff_down[l])
    return x
```

```python
import functools
import math

import jax
import jax.numpy as jnp
from jax import lax
from jax.experimental import pallas as pl
from jax.experimental.pallas import tpu as pltpu

F32 = jnp.float32
BF16 = jnp.bfloat16

SGU_GROUPS = 16
CHUNK = 128
DIFF_HEADS = 16
CONV_WIDTH = 3
EPS = 1e-6
NEG_INF = -1e30

LANES = 128
SUBLANES = 8
ROW_SUB = 256
VMEM_LIMIT_BYTES = 60000 * 1024
ADA_ROWS = 16


def _params(semantics):
    return pltpu.CompilerParams(dimension_semantics=semantics, vmem_limit_bytes=VMEM_LIMIT_BYTES)


def _gelu(x):
    return 0.5 * x * (1.0 + lax.erf(x * (1.0 / math.sqrt(2.0))))


def _ada_kernel(c_ref, w_ref, b_ref, o_ref):
    c = c_ref[...]
    c_act = (c * jax.nn.sigmoid(c)).astype(BF16)
    acc = jnp.dot(c_act, w_ref[...].astype(BF16), preferred_element_type=F32)
    o_ref[...] = acc + b_ref[...]


def _ada(c_pad, w_ada, b_ada, tn=512):
    rows, d = c_pad.shape
    n = w_ada.shape[1]
    return pl.pallas_call(
        _ada_kernel,
        grid=(n // tn,),
        in_specs=[pl.BlockSpec((rows, d), lambda j: (0, 0)),
                  pl.BlockSpec((d, tn), lambda j: (0, j)),
                  pl.BlockSpec((1, tn), lambda j: (0, j))],
        out_specs=pl.BlockSpec((rows, tn), lambda j: (0, j)),
        out_shape=jax.ShapeDtypeStruct((rows, n), F32),
        compiler_params=_params(("arbitrary",)),
        name="ada",
    )(c_pad, w_ada, b_ada)


def _norm_mod_kernel(x_ref, g_ref, scale_ref, shift_ref, o_ref):
    x = x_ref[0]
    ms = jnp.mean(x * x, axis=-1, keepdims=True)
    y = x * lax.rsqrt(ms + EPS) * g_ref[...]
    o_ref[0] = (y * (1.0 + scale_ref[0]) + shift_ref[0]).astype(BF16)


def _norm_mod(x, g, mod, scale_idx, shift_idx, tr=256):
    b, s, d = x.shape
    return pl.pallas_call(
        _norm_mod_kernel,
        grid=(b, s // tr),
        in_specs=[pl.BlockSpec((1, tr, d), lambda i, r: (i, r, 0)),
                  pl.BlockSpec((1, d), lambda i, r: (0, 0)),
                  pl.BlockSpec((1, 1, d), lambda i, r: (i * 6 + scale_idx, 0, 0)),
                  pl.BlockSpec((1, 1, d), lambda i, r: (i * 6 + shift_idx, 0, 0))],
        out_specs=pl.BlockSpec((1, tr, d), lambda i, r: (i, r, 0)),
        out_shape=jax.ShapeDtypeStruct((b, s, d), BF16),
        compiler_params=_params(("arbitrary", "arbitrary")),
        name="norm_mod",
    )(x, g, mod, mod)


def _proj_kernel(h_ref, *refs, n_w, tw, epilogue, init):
    w_refs, rest, wbf_ref = refs[:n_w], refs[n_w:-1], refs[-1]
    if init is not None:
        init(*rest)
    for s, w_ref in enumerate(w_refs):
        wbf_ref[:, s * tw:(s + 1) * tw] = w_ref[...].astype(BF16)
    carry = None
    for r0 in range(0, h_ref.shape[0], ROW_SUB):
        rows = slice(r0, r0 + ROW_SUB)
        acc = jnp.dot(h_ref[rows, :], wbf_ref[...], preferred_element_type=F32)
        carry = epilogue(acc, rows, carry, *rest)


def _proj(h, w, w_offsets, tw, n_tiles, epilogue, extra, extra_specs, out_shape, out_specs, name,
          init=None):
    r, tm, k = h.shape
    n_w = len(w_offsets)
    h_spec = pl.BlockSpec((tm, k), lambda i, j: (i, 0), pipeline_mode=pl.Buffered(1))
    w_specs = [pl.BlockSpec((k, tw), functools.partial(lambda i, j, off: (0, off + j), off=off))
               for off in w_offsets]
    return pl.pallas_call(
        functools.partial(_proj_kernel, n_w=n_w, tw=tw, epilogue=epilogue, init=init),
        grid=(r, n_tiles),
        in_specs=[h_spec] + w_specs + list(extra_specs),
        out_specs=out_specs,
        out_shape=out_shape,
        scratch_shapes=[pltpu.VMEM((k, n_w * tw), BF16)],
        compiler_params=_params(("arbitrary", "arbitrary")),
        name=name,
    )(h.reshape(r * tm, k), *([w] * n_w), *extra)


def _ug_epilogue(acc, rows, _, o_ref):
    tw = acc.shape[1] // 2
    o_ref[0, rows, :] = (_gelu(acc[:, :tw]) * jax.nn.sigmoid(acc[:, tw:])).astype(o_ref.dtype)


def _v_epilogue(acc, rows, _, o_ref, ssq_ref):
    g = _gelu(acc)
    o_ref[0, rows, :] = g.astype(o_ref.dtype)
    sq = g * g
    part = sq[:, :LANES]
    for k in range(1, acc.shape[1] // LANES):
        part = part + sq[:, k * LANES:(k + 1) * LANES]

    ssq_ref[0, rows, :] += part


def _v_init(o_ref, ssq_ref):
    @pl.when(pl.program_id(1) == 0)
    def _():
        ssq_ref[...] = jnp.zeros_like(ssq_ref)


def _qk_epilogue(acc, rows, _, g_ref, o_ref):
    for k in range(acc.shape[1] // LANES):
        cols = slice(k * LANES, (k + 1) * LANES)
        z = acc[:, cols]
        ms = jnp.mean(z * z, axis=-1, keepdims=True)
        o_ref[0, rows, cols] = (z * lax.rsqrt(ms + EPS) * g_ref[:, cols]).astype(BF16)


def _transpose_epilogue(acc, rows, _, o_ref):
    o_ref[0, :, rows] = acc.T.astype(BF16)


def _sigmoid_epilogue(acc, rows, _, o_ref):
    o_ref[0, rows, :] = jax.nn.sigmoid(acc).astype(o_ref.dtype)


def _residual_epilogue(acc, rows, _, x_ref, gate_ref, o_ref):
    o_ref[0, rows, :] = x_ref[0, rows, :] + gate_ref[0] * acc


def _cast_w_down(cwg_ref, cwv_ref, cbg_ref, cbv_ref, wd_ref, o_ref, wd_bf16_ref):
    wd_bf16_ref[...] = wd_ref[...].astype(BF16)


def _conv_swiglu_epilogue(acc, rows, prev_tail, cwg_ref, cwv_ref, cbg_ref, cbv_ref, wd_ref, o_ref,
                          wd_bf16_ref):
    tw = acc.shape[1] // 2
    if prev_tail is None:
        prev_tail = jnp.zeros((SUBLANES, acc.shape[1]), F32)
    row = lax.broadcasted_iota(jnp.int32, (SUBLANES, acc.shape[1]), 0)

    def shifted(k):
        rolled = pltpu.roll(acc, k, 0)
        head = jnp.where(row >= k, rolled[:SUBLANES], pltpu.roll(prev_tail, k, 0))
        return jnp.concatenate([head, rolled[SUBLANES:]], axis=0)

    a1, a2 = shifted(1), shifted(2)

    def conv(lo, cw_ref, cb_ref):
        sl = slice(lo, lo + tw)
        return (cw_ref[0:1, :] * a2[:, sl] + cw_ref[1:2, :] * a1[:, sl]
                + cw_ref[2:3, :] * acc[:, sl] + cb_ref[...])

    gate = conv(0, cwg_ref, cbg_ref)
    val = conv(tw, cwv_ref, cbv_ref)
    o_ref[0, rows, :] = (gate * jax.nn.sigmoid(gate) * val).astype(BF16)
    return acc[-SUBLANES:]


def _spatial_gating(rows, ws, ug_ref, gv_ref, ssq_ref, sgug_ref, bs_ref):
    d_total = gv_ref.shape[2] * SGU_GROUPS
    ms = jnp.sum(ssq_ref[0, rows, :], axis=-1, keepdims=True) * (1.0 / d_total)
    vn = (gv_ref[0, rows, :] * lax.rsqrt(ms + EPS) * sgug_ref[...]).astype(BF16)
    mixed = jnp.dot(ws, vn, preferred_element_type=F32) + bs_ref[0]
    return ug_ref[0, rows, :] * mixed


def _mixer_kernel(q_ref, k_ref, vt_ref, sgb_ref, ug_ref, gv_ref, ssq_ref, sgug_ref, ws_ref, bs_ref,
                  lam_ref, subg_ref, o_ref, *, tq, hd, lambda_init):
    s_len = q_ref.shape[1]
    ws_row = lax.broadcasted_iota(jnp.int32, (CHUNK, CHUNK), 0)
    ws_col = lax.broadcasted_iota(jnp.int32, (CHUNK, CHUNK), 1)
    ws = jnp.where(ws_row >= ws_col, ws_ref[0], 0.0).astype(BF16)
    head = pl.program_id(1)
    log2e = math.log2(math.e)
    c1 = hd ** -0.5 * log2e
    slope2 = log2e * jnp.exp2(jnp.full((1, 1), -8.0 / DIFF_HEADS, F32) * (head + 1).astype(F32))
    kv_i = lax.broadcasted_iota(jnp.int32, (tq, tq), 0)
    q_i = lax.broadcasted_iota(jnp.int32, (tq, tq), 1)
    rel_bias = slope2 * (q_i - kv_i).astype(F32)
    causal = q_i >= kv_i

    lv = lam_ref[...]
    lam = (jnp.exp(jnp.sum(lv[0:1] * lv[1:2], axis=-1, keepdims=True))
           - jnp.exp(jnp.sum(lv[2:3] * lv[3:4], axis=-1, keepdims=True)) + lambda_init)

    for qi in range(s_len // tq):
        kv_len = (qi + 1) * tq
        q_rows = slice(qi * tq, (qi + 1) * tq)
        chunk_off = [slope2 * float(tq * (qi - r)) for r in range(qi + 1)]
        probs, norms = [], []
        for c in range(2):
            qc = q_ref[0, q_rows, c * hd:(c + 1) * hd]
            kc = k_ref[0, :kv_len, c * hd:(c + 1) * hd]
            s_all = lax.dot_general(kc, qc, (((1,), (1,)), ((), ())),
                                    preferred_element_type=F32)
            t, m = [], None
            for r in range(qi + 1):
                tr = s_all[r * tq:(r + 1) * tq] * c1 - rel_bias
                if r == qi:
                    tr = jnp.where(causal, tr, NEG_INF)
                mr = jnp.max(tr, axis=0, keepdims=True) - chunk_off[r]
                m = mr if m is None else jnp.maximum(m, mr)
                t.append(tr)
            p, l = [], None
            for r in range(qi + 1):
                pr = jnp.exp2(t[r] - (m + chunk_off[r]))
                lr = jnp.sum(pr, axis=0, keepdims=True)
                l = lr if l is None else l + lr
                p.append(pr)
            probs.append(p)
            norms.append(l)
        w0 = 1.0 / norms[0]
        w1 = lam / norms[1]
        a = jnp.concatenate([(probs[0][r] * w0 - probs[1][r] * w1).astype(BF16)
                             for r in range(qi + 1)], axis=0)
        o = jnp.dot(vt_ref[0, :, :kv_len], a, preferred_element_type=F32).T
        ms = jnp.mean(o * o, axis=-1, keepdims=True)
        yb = o * lax.rsqrt(ms + EPS) * subg_ref[...] * (1.0 - lambda_init)
        ya = jnp.concatenate(
            [_spatial_gating(slice(r0, r0 + CHUNK), ws, ug_ref, gv_ref, ssq_ref, sgug_ref, bs_ref)
             for r0 in range(qi * tq, (qi + 1) * tq, CHUNK)], axis=0)
        o_ref[0, q_rows, :] = (sgb_ref[0, q_rows, :] * yb + ya).astype(BF16)


def _mixer(qk, vt, sgb, ug, gv, ssq, sgu_g, w_spatial, b_spatial, lam_rows, subln_g, lambda_init,
           tq=256):
    b, d, s = vt.shape
    vd = d // DIFF_HEADS
    hd = vd // 2
    assert SGU_GROUPS == DIFF_HEADS and tq % CHUNK == 0
    head_spec = pl.BlockSpec((1, s, vd), lambda i, h: (i, 0, h))
    return pl.pallas_call(
        functools.partial(_mixer_kernel, tq=tq, hd=hd, lambda_init=lambda_init),
        grid=(b, DIFF_HEADS),
        in_specs=[head_spec,
                  pl.BlockSpec((1, s, vd), lambda i, h: (i, 0, DIFF_HEADS + h)),
                  pl.BlockSpec((1, vd, s), lambda i, h: (i, h, 0)),
                  head_spec, head_spec, head_spec,
                  pl.BlockSpec((1, s, LANES), lambda i, h: (i, 0, 0)),
                  pl.BlockSpec((1, vd), lambda i, h: (0, h)),
                  pl.BlockSpec((1, CHUNK, CHUNK), lambda i, h: (h, 0, 0)),
                  pl.BlockSpec((1, CHUNK, 1), lambda i, h: (h, 0, 0)),
                  pl.BlockSpec((4, hd), lambda i, h: (0, 0)),
                  pl.BlockSpec((1, vd), lambda i, h: (0, 0))],
        out_specs=head_spec,
        out_shape=jax.ShapeDtypeStruct((b, s, d), BF16),
        compiler_params=_params(("arbitrary", "arbitrary")),
        name="mixer",
    )(qk, qk, vt, sgb, ug, gv, ssq, sgu_g, w_spatial, b_spatial[:, :, None], lam_rows, subln_g)


def _down_kernel(g_ref, w_ref, x_ref, gate_ref, o_ref):
    acc = jnp.dot(g_ref[...], w_ref[...], preferred_element_type=F32)
    o_ref[...] = x_ref[...] + gate_ref[0] * acc


def _down(g, w_bf16, x1, mod, gate_idx, seq, tm=512, tn=512):
    m, f = g.shape
    n = w_bf16.shape[1]
    tiles_per_seq = seq // tm
    return pl.pallas_call(
        _down_kernel,
        grid=(m // tm, n // tn),
        in_specs=[pl.BlockSpec((tm, f), lambda i, j: (i, 0)),
                  pl.BlockSpec((f, tn), lambda i, j: (0, j)),
                  pl.BlockSpec((tm, tn), lambda i, j: (i, j)),
                  pl.BlockSpec((1, 1, tn), lambda i, j: ((i // tiles_per_seq) * 6 + gate_idx, 0, j))],
        out_specs=pl.BlockSpec((tm, tn), lambda i, j: (i, j)),
        out_shape=jax.ShapeDtypeStruct((m, n), F32),
        compiler_params=_params(("arbitrary", "arbitrary")),
        name="down",
    )(g, w_bf16, x1, mod)


MXU_TILE = 256
PAIR_TW = MXU_TILE
WIDE_TW = 2 * MXU_TILE


def _layer(x, c_pad, lambda_init, w_ada, b_ada, norm1_g, norm2_g, w_in, sgu_norm_g, w_spatial,
           b_spatial, q_norm_g, k_norm_g, lam_rows, subln_g, w_out, w_ff_up, conv_w, conv_b,
           w_ff_down):
    b, s, d = x.shape
    f = w_ff_down.shape[0]
    mod = _ada(c_pad, w_ada, b_ada)[:b].reshape(b * 6, 1, d)

    def out3(dtype, n=d):
        return jax.ShapeDtypeStruct((b, s, n), dtype)

    def tile_spec(width):
        return pl.BlockSpec((1, s, width), lambda i, j: (i, 0, j))

    h = _norm_mod(x, norm1_g, mod, scale_idx=1, shift_idx=0)

    pair_seg, wide_seg = d // PAIR_TW, d // WIDE_TW
    ug = _proj(h, w_in, (0 * pair_seg, 5 * pair_seg), PAIR_TW, pair_seg, _ug_epilogue, (), (),
               out3(BF16), tile_spec(PAIR_TW), "proj_ug")
    gv, ssq = _proj(h, w_in, (1 * wide_seg,), WIDE_TW, wide_seg, _v_epilogue, (), (),
                    (out3(BF16), out3(F32, LANES)),
                    (tile_spec(WIDE_TW), pl.BlockSpec((1, s, LANES), lambda i, j: (i, 0, 0))),
                    "proj_v", init=_v_init)
    hd = q_norm_g.shape[-1]
    qk_g = jnp.concatenate([jnp.tile(q_norm_g, d // hd), jnp.tile(k_norm_g, d // hd)])[None]
    qk = _proj(h, w_in, (2 * wide_seg,), WIDE_TW, 2 * wide_seg, _qk_epilogue, (qk_g,),
               (pl.BlockSpec((1, WIDE_TW), lambda i, j: (0, j)),),
               out3(BF16, 2 * d), tile_spec(WIDE_TW), "proj_qk")
    vbt = _proj(h, w_in, (4 * wide_seg,), WIDE_TW, wide_seg, _transpose_epilogue, (), (),
                jax.ShapeDtypeStruct((b, d, s), BF16),
                pl.BlockSpec((1, WIDE_TW, s), lambda i, j: (i, j, 0)), "proj_vb")
    sgb = _proj(h, w_in, (6 * wide_seg,), WIDE_TW, wide_seg, _sigmoid_epilogue, (), (),
                out3(BF16), tile_spec(WIDE_TW), "proj_gb")

    y = _mixer(qk, vbt, sgb, ug, gv, ssq, sgu_norm_g[None], w_spatial, b_spatial, lam_rows,
               subln_g[None], lambda_init)

    tm = s
    rows3 = (b * s // tm, tm, d)
    x1 = _proj(y.reshape(rows3), w_out, (0,), WIDE_TW, d // WIDE_TW, _residual_epilogue,
               (x.reshape(rows3), mod),
               (pl.BlockSpec((1, tm, WIDE_TW), lambda i, j: (i, 0, j)),
                pl.BlockSpec((1, 1, WIDE_TW), lambda i, j: ((i // (s // tm)) * 6 + 2, 0, j))),
               jax.ShapeDtypeStruct(rows3, F32),
               pl.BlockSpec((1, tm, WIDE_TW), lambda i, j: (i, 0, j)), "proj_out")
    x1 = x1.reshape(b, s, d)

    h2 = _norm_mod(x1, norm2_g, mod, scale_idx=4, shift_idx=3)
    val_off = f // PAIR_TW
    conv_b2 = conv_b[None]
    wd_rows = f // (b * val_off)
    assert wd_rows * b * val_off == f and wd_rows % (2 * SUBLANES) == 0
    wd_spec = pl.BlockSpec((wd_rows, d), lambda i, j: (i * val_off + j, 0))
    g, w_down_bf16 = _proj(
        h2, w_ff_up, (0, val_off), PAIR_TW, val_off, _conv_swiglu_epilogue,
        (conv_w, conv_w, conv_b2, conv_b2, w_ff_down),
        (pl.BlockSpec((CONV_WIDTH, PAIR_TW), lambda i, j: (0, j)),
         pl.BlockSpec((CONV_WIDTH, PAIR_TW), lambda i, j: (0, val_off + j)),
         pl.BlockSpec((1, PAIR_TW), lambda i, j: (0, j)),
         pl.BlockSpec((1, PAIR_TW), lambda i, j: (0, val_off + j)),
         wd_spec),
        (out3(BF16, f), jax.ShapeDtypeStruct((f, d), BF16)), (tile_spec(PAIR_TW), wd_spec),
        "proj_up", init=_cast_w_down)

    out = _down(g.reshape(b * s, f), w_down_bf16, x1.reshape(b * s, d), mod, gate_idx=5, seq=s)
    return out.reshape(b, s, d)


def kernel(x, c, w_ada, b_ada, norm1_g, norm2_g, w_in, sgu_norm_g, w_spatial, b_spatial, q_norm_g, k_norm_g, lambda_q1, lambda_k1, lambda_q2, lambda_k2, subln_g, w_out, w_ff_up, conv_w, conv_b, w_ff_down):
    depth = w_in.shape[0]
    c_pad = jnp.pad(c, ((0, ADA_ROWS - c.shape[0]), (0, 0)))
    for l in range(depth):
        lam_rows = jnp.stack([lambda_q1[l], lambda_k1[l], lambda_q2[l], lambda_k2[l]])
        x = _layer(x, c_pad, 0.8 - 0.6 * math.exp(-0.3 * l), w_ada[l], b_ada[l][None],
                   norm1_g[l][None], norm2_g[l][None], w_in[l], sgu_norm_g[l], w_spatial[l],
                   b_spatial[l], q_norm_g[l], k_norm_g[l], lam_rows, subln_g[l], w_out[l],
                   w_ff_up[l], conv_w[l], conv_b[l], w_ff_down[l])
    return x
```

```python
import functools
import itertools
import math

import jax
import jax.numpy as jnp
from jax import lax
from jax.experimental import pallas as pl
from jax.experimental.pallas import tpu as pltpu

F32 = jnp.float32
BF16 = jnp.bfloat16

SGU_GROUPS = 16
CHUNK = 128
DIFF_HEADS = 16
CONV_WIDTH = 3
EPS = 1e-6
NEG_INF = -1e30

LANES = 128
SUBLANES = 8
ROW_SUB = 256
K_PAD = 128
VMEM_LIMIT_BYTES = 60000 * 1024
ADA_ROWS = 16


def _params(semantics):
    return pltpu.CompilerParams(dimension_semantics=semantics, vmem_limit_bytes=VMEM_LIMIT_BYTES)


def _gelu(x):
    return 0.5 * x * (1.0 + lax.erf(x * (1.0 / math.sqrt(2.0))))


def _ada_kernel(c_ref, w_ref, b_ref, o_ref):
    c = c_ref[...]
    c_act = (c * jax.nn.sigmoid(c)).astype(BF16)
    acc = jnp.dot(c_act, w_ref[...].astype(BF16), preferred_element_type=F32)
    o_ref[...] = acc + b_ref[...]


def _ada(c_pad, w_ada, b_ada, tn=512):
    rows, d = c_pad.shape
    n = w_ada.shape[1]
    return pl.pallas_call(
        _ada_kernel,
        grid=(n // tn,),
        in_specs=[pl.BlockSpec((rows, d), lambda j: (0, 0)),
                  pl.BlockSpec((d, tn), lambda j: (0, j)),
                  pl.BlockSpec((1, tn), lambda j: (0, j))],
        out_specs=pl.BlockSpec((rows, tn), lambda j: (0, j)),
        out_shape=jax.ShapeDtypeStruct((rows, n), F32),
        compiler_params=_params(("arbitrary",)),
        name="ada",
    )(c_pad, w_ada, b_ada)


def _norm_mod_kernel(x_ref, g_ref, scale_ref, shift_ref, o_ref):
    x = x_ref[0]
    ms = jnp.mean(x * x, axis=-1, keepdims=True)
    y = x * lax.rsqrt(ms + EPS) * g_ref[...]
    d = x.shape[1]
    o_ref[0, :, :d] = (y * (1.0 + scale_ref[0]) + shift_ref[0]).astype(BF16)
    o_ref[0, :, d:] = jnp.zeros((x.shape[0], o_ref.shape[2] - d), BF16)


def _norm_mod(x, g, mod, scale_idx, shift_idx, tr=256):
    b, s, d = x.shape
    return pl.pallas_call(
        _norm_mod_kernel,
        grid=(b, s // tr),
        in_specs=[pl.BlockSpec((1, tr, d), lambda i, r: (i, r, 0)),
                  pl.BlockSpec((1, d), lambda i, r: (0, 0)),
                  pl.BlockSpec((1, 1, d), lambda i, r: (i * 6 + scale_idx, 0, 0)),
                  pl.BlockSpec((1, 1, d), lambda i, r: (i * 6 + shift_idx, 0, 0))],
        out_specs=pl.BlockSpec((1, tr, d + K_PAD), lambda i, r: (i, r, 0)),
        out_shape=jax.ShapeDtypeStruct((b, s, d + K_PAD), BF16),
        compiler_params=_params(("arbitrary", "arbitrary")),
        name="norm_mod",
    )(x, g, mod, mod)


def _proj_kernel(h_hbm, *refs, n_w, tw, epilogue, init):
    w_refs, rest = refs[:n_w], refs[n_w:-3]
    wbf_ref, h_ref, sems = refs[-3:]
    tm, half = h_ref.shape[0], h_ref.shape[0] // 2
    i, j = pl.program_id(0), pl.program_id(1)
    n_i, n_j = pl.num_programs(0), pl.num_programs(1)

    def h_copy(tile, part):
        src_rows = pl.ds(pl.multiple_of(tile * tm + part * half, half), half)
        return pltpu.make_async_copy(h_hbm.at[src_rows, :],
                                     h_ref.at[part * half:(part + 1) * half, :], sems.at[part])

    def body(wait_current, prefetch_next):
        if init is not None:
            init(*rest)
        for s, w_ref in enumerate(w_refs):
            wbf_ref[:, s * tw:(s + 1) * tw] = w_ref[...].astype(BF16)
        carry, r0 = None, 0
        for size in _row_splits(tm):
            for part in range(2):
                if wait_current and r0 == part * half:
                    h_copy(i, part).wait()
            rows = slice(r0, r0 + size)
            acc = jnp.dot(h_ref[rows, :wbf_ref.shape[0]], wbf_ref[...],
                          preferred_element_type=F32)
            r0 += size
            for part in range(2):
                if prefetch_next and r0 == (part + 1) * half:
                    h_copy(i + 1, part).start()
            carry = epilogue(acc, rows, carry, *rest)

    @pl.when(jnp.logical_and(i == 0, j == 0))
    def _():
        h_copy(0, 0).start()
        h_copy(0, 1).start()

    is_first = j == 0
    is_last = jnp.logical_and(j == n_j - 1, i < n_i - 1)
    pl.when(is_first)(lambda: body(True, False))
    pl.when(is_last)(lambda: body(False, True))
    pl.when(jnp.logical_not(jnp.logical_or(is_first, is_last)))(lambda: body(False, False))


def _row_splits(tm):
    return [ROW_SUB] * (tm // ROW_SUB - 1) + [ROW_SUB // 2] * 2


def _proj(h, w, w_offsets, tw, n_tiles, epilogue, extra, extra_specs, out_shape, out_specs, name,
          init=None):
    r, tm, kp = h.shape
    k = w.shape[0]
    n_w = len(w_offsets)
    splits = _row_splits(tm)
    assert n_tiles >= 2 and sum(splits) == tm and tm // 2 in itertools.accumulate(splits)
    w_specs = [pl.BlockSpec((k, tw), functools.partial(lambda i, j, off: (0, off + j), off=off))
               for off in w_offsets]
    return pl.pallas_call(
        functools.partial(_proj_kernel, n_w=n_w, tw=tw, epilogue=epilogue, init=init),
        grid=(r, n_tiles),
        in_specs=[pl.BlockSpec(memory_space=pl.ANY)] + w_specs + list(extra_specs),
        out_specs=out_specs,
        out_shape=out_shape,
        scratch_shapes=[pltpu.VMEM((k, n_w * tw), BF16), pltpu.VMEM((tm, kp), BF16),
                        pltpu.SemaphoreType.DMA((2,))],
        compiler_params=_params(("arbitrary", "arbitrary")),
        name=name,
    )(h.reshape(r * tm, kp), *([w] * n_w), *extra)


def _ug_epilogue(acc, rows, _, o_ref):
    tw = acc.shape[1] // 2
    o_ref[0, rows, :] = (_gelu(acc[:, :tw]) * jax.nn.sigmoid(acc[:, tw:])).astype(o_ref.dtype)


def _v_epilogue(acc, rows, _, o_ref, ssq_ref):
    g = _gelu(acc)
    o_ref[0, rows, :] = g.astype(o_ref.dtype)
    sq = g * g
    part = sq[:, :LANES]
    for k in range(1, acc.shape[1] // LANES):
        part = part + sq[:, k * LANES:(k + 1) * LANES]

    ssq_ref[0, rows, :] += part


def _v_init(o_ref, ssq_ref):
    @pl.when(pl.program_id(1) == 0)
    def _():
        ssq_ref[...] = jnp.zeros_like(ssq_ref)


def _qk_epilogue(acc, rows, _, g_ref, o_ref):
    for k in range(acc.shape[1] // LANES):
        cols = slice(k * LANES, (k + 1) * LANES)
        z = acc[:, cols]
        ms = jnp.mean(z * z, axis=-1, keepdims=True)
        o_ref[0, rows, cols] = (z * lax.rsqrt(ms + EPS) * g_ref[:, cols]).astype(BF16)


def _transpose_epilogue(acc, rows, _, o_ref):
    o_ref[0, :, rows] = acc.T.astype(BF16)


def _sigmoid_epilogue(acc, rows, _, o_ref):
    o_ref[0, rows, :] = jax.nn.sigmoid(acc).astype(o_ref.dtype)


def _residual_epilogue(acc, rows, _, x_ref, gate_ref, o_ref):
    o_ref[0, rows, :] = x_ref[0, rows, :] + gate_ref[0] * acc


def _cast_w_down(cwg_ref, cwv_ref, cbg_ref, cbv_ref, wd_ref, o_ref, wd_bf16_ref):
    wd_bf16_ref[...] = wd_ref[...].astype(BF16)


def _conv_swiglu_epilogue(acc, rows, prev_tail, cwg_ref, cwv_ref, cbg_ref, cbv_ref, wd_ref, o_ref,
                          wd_bf16_ref):
    tw = acc.shape[1] // 2
    if prev_tail is None:
        prev_tail = jnp.zeros((SUBLANES, acc.shape[1]), F32)
    row = lax.broadcasted_iota(jnp.int32, (SUBLANES, acc.shape[1]), 0)

    def shifted(k):
        rolled = pltpu.roll(acc, k, 0)
        head = jnp.where(row >= k, rolled[:SUBLANES], pltpu.roll(prev_tail, k, 0))
        return jnp.concatenate([head, rolled[SUBLANES:]], axis=0)

    a1, a2 = shifted(1), shifted(2)

    def conv(lo, cw_ref, cb_ref):
        sl = slice(lo, lo + tw)
        return (cw_ref[0:1, :] * a2[:, sl] + cw_ref[1:2, :] * a1[:, sl]
                + cw_ref[2:3, :] * acc[:, sl] + cb_ref[...])

    gate = conv(0, cwg_ref, cbg_ref)
    val = conv(tw, cwv_ref, cbv_ref)
    o_ref[0, rows, :] = (gate * jax.nn.sigmoid(gate) * val).astype(BF16)
    return acc[-SUBLANES:]


def _spatial_gating(rows, ws, ug_ref, gv_ref, ssq_ref, sgug_ref, bs_ref):
    d_total = gv_ref.shape[2] * SGU_GROUPS
    ms = jnp.sum(ssq_ref[0, rows, :], axis=-1, keepdims=True) * (1.0 / d_total)
    vn = (gv_ref[0, rows, :] * lax.rsqrt(ms + EPS) * sgug_ref[...]).astype(BF16)
    mixed = jnp.dot(ws, vn, preferred_element_type=F32) + bs_ref[0]
    return ug_ref[0, rows, :] * mixed


def _mixer_kernel(q_ref, k_ref, vt_ref, sgb_ref, ug_ref, gv_ref, ssq_ref, sgug_ref, ws_ref, bs_ref,
                  lam_ref, subg_ref, o_ref, *, tq, hd, lambda_init):
    s_len = q_ref.shape[1]
    ws_row = lax.broadcasted_iota(jnp.int32, (CHUNK, CHUNK), 0)
    ws_col = lax.broadcasted_iota(jnp.int32, (CHUNK, CHUNK), 1)
    ws = jnp.where(ws_row >= ws_col, ws_ref[0], 0.0).astype(BF16)
    head = pl.program_id(1)
    log2e = math.log2(math.e)
    c1 = hd ** -0.5 * log2e
    slope2 = log2e * jnp.exp2(jnp.full((1, 1), -8.0 / DIFF_HEADS, F32) * (head + 1).astype(F32))
    kv_i = lax.broadcasted_iota(jnp.int32, (tq, tq), 0)
    q_i = lax.broadcasted_iota(jnp.int32, (tq, tq), 1)
    rel_bias = slope2 * (q_i - kv_i).astype(F32)
    causal = q_i >= kv_i

    lv = lam_ref[...]
    lam = (jnp.exp(jnp.sum(lv[0:1] * lv[1:2], axis=-1, keepdims=True))
           - jnp.exp(jnp.sum(lv[2:3] * lv[3:4], axis=-1, keepdims=True)) + lambda_init)

    for qi in range(s_len // tq):
        kv_len = (qi + 1) * tq
        q_rows = slice(qi * tq, (qi + 1) * tq)
        chunk_off = [slope2 * float(tq * (qi - r)) for r in range(qi + 1)]
        probs, norms = [], []
        for c in range(2):
            qc = q_ref[0, q_rows, c * hd:(c + 1) * hd]
            kc = k_ref[0, :kv_len, c * hd:(c + 1) * hd]
            s_all = lax.dot_general(kc, qc, (((1,), (1,)), ((), ())),
                                    preferred_element_type=F32)
            t, m = [], None
            for r in range(qi + 1):
                tr = s_all[r * tq:(r + 1) * tq] * c1 - rel_bias
                if r == qi:
                    tr = jnp.where(causal, tr, NEG_INF)
                mr = jnp.max(tr, axis=0, keepdims=True) - chunk_off[r]
                m = mr if m is None else jnp.maximum(m, mr)
                t.append(tr)
            p, l = [], None
            for r in range(qi + 1):
                pr = jnp.exp2(t[r] - (m + chunk_off[r]))
                lr = jnp.sum(pr, axis=0, keepdims=True)
                l = lr if l is None else l + lr
                p.append(pr)
            probs.append(p)
            norms.append(l)
        w0 = 1.0 / norms[0]
        w1 = lam / norms[1]
        a = jnp.concatenate([(probs[0][r] * w0 - probs[1][r] * w1).astype(BF16)
                             for r in range(qi + 1)], axis=0)
        o = jnp.dot(vt_ref[0, :, :kv_len], a, preferred_element_type=F32).T
        ms = jnp.mean(o * o, axis=-1, keepdims=True)
        yb = o * lax.rsqrt(ms + EPS) * subg_ref[...] * (1.0 - lambda_init)
        ya = jnp.concatenate(
            [_spatial_gating(slice(r0, r0 + CHUNK), ws, ug_ref, gv_ref, ssq_ref, sgug_ref, bs_ref)
             for r0 in range(qi * tq, (qi + 1) * tq, CHUNK)], axis=0)
        o_ref[0, q_rows, :] = (sgb_ref[0, q_rows, :] * yb + ya).astype(BF16)


def _mixer(qk, vt, sgb, ug, gv, ssq, sgu_g, w_spatial, b_spatial, lam_rows, subln_g, lambda_init,
           tq=256):
    b, d, s = vt.shape
    vd = d // DIFF_HEADS
    hd = vd // 2
    assert SGU_GROUPS == DIFF_HEADS and tq % CHUNK == 0
    head_spec = pl.BlockSpec((1, s, vd), lambda i, h: (i, 0, h))
    return pl.pallas_call(
        functools.partial(_mixer_kernel, tq=tq, hd=hd, lambda_init=lambda_init),
        grid=(b, DIFF_HEADS),
        in_specs=[head_spec,
                  pl.BlockSpec((1, s, vd), lambda i, h: (i, 0, DIFF_HEADS + h)),
                  pl.BlockSpec((1, vd, s), lambda i, h: (i, h, 0)),
                  head_spec, head_spec, head_spec,
                  pl.BlockSpec((1, s, LANES), lambda i, h: (i, 0, 0)),
                  pl.BlockSpec((1, vd), lambda i, h: (0, h)),
                  pl.BlockSpec((1, CHUNK, CHUNK), lambda i, h: (h, 0, 0)),
                  pl.BlockSpec((1, CHUNK, 1), lambda i, h: (h, 0, 0)),
                  pl.BlockSpec((4, hd), lambda i, h: (0, 0)),
                  pl.BlockSpec((1, vd), lambda i, h: (0, 0))],
        out_specs=head_spec,
        out_shape=jax.ShapeDtypeStruct((b, s, d), BF16),
        compiler_params=_params(("arbitrary", "arbitrary")),
        name="mixer",
    )(qk, qk, vt, sgb, ug, gv, ssq, sgu_g, w_spatial, b_spatial[:, :, None], lam_rows, subln_g)


def _down_kernel(g_ref, w_ref, x_ref, gate_ref, o_ref):
    acc = jnp.dot(g_ref[...], w_ref[...], preferred_element_type=F32)
    o_ref[...] = x_ref[...] + gate_ref[0] * acc


def _down(g, w_bf16, x1, mod, gate_idx, seq, tm=512, tn=512):
    m, f = g.shape
    n = w_bf16.shape[1]
    tiles_per_seq = seq // tm
    return pl.pallas_call(
        _down_kernel,
        grid=(m // tm, n // tn),
        in_specs=[pl.BlockSpec((tm, f), lambda i, j: (i, 0)),
                  pl.BlockSpec((f, tn), lambda i, j: (0, j)),
                  pl.BlockSpec((tm, tn), lambda i, j: (i, j)),
                  pl.BlockSpec((1, 1, tn), lambda i, j: ((i // tiles_per_seq) * 6 + gate_idx, 0, j))],
        out_specs=pl.BlockSpec((tm, tn), lambda i, j: (i, j)),
        out_shape=jax.ShapeDtypeStruct((m, n), F32),
        compiler_params=_params(("arbitrary", "arbitrary")),
        name="down",
    )(g, w_bf16, x1, mod)


MXU_TILE = 256
PAIR_TW = MXU_TILE
WIDE_TW = 2 * MXU_TILE


def _layer(x, c_pad, lambda_init, w_ada, b_ada, norm1_g, norm2_g, w_in, sgu_norm_g, w_spatial,
           b_spatial, q_norm_g, k_norm_g, lam_rows, subln_g, w_out, w_ff_up, conv_w, conv_b,
           w_ff_down):
    b, s, d = x.shape
    f = w_ff_down.shape[0]
    mod = _ada(c_pad, w_ada, b_ada)[:b].reshape(b * 6, 1, d)

    def out3(dtype, n=d):
        return jax.ShapeDtypeStruct((b, s, n), dtype)

    def tile_spec(width):
        return pl.BlockSpec((1, s, width), lambda i, j: (i, 0, j))

    h = _norm_mod(x, norm1_g, mod, scale_idx=1, shift_idx=0)

    pair_seg, wide_seg = d // PAIR_TW, d // WIDE_TW
    ug = _proj(h, w_in, (0 * pair_seg, 5 * pair_seg), PAIR_TW, pair_seg, _ug_epilogue, (), (),
               out3(BF16), tile_spec(PAIR_TW), "proj_ug")
    gv, ssq = _proj(h, w_in, (1 * wide_seg,), WIDE_TW, wide_seg, _v_epilogue, (), (),
                    (out3(BF16), out3(F32, LANES)),
                    (tile_spec(WIDE_TW), pl.BlockSpec((1, s, LANES), lambda i, j: (i, 0, 0))),
                    "proj_v", init=_v_init)
    hd = q_norm_g.shape[-1]
    qk_g = jnp.concatenate([jnp.tile(q_norm_g, d // hd), jnp.tile(k_norm_g, d // hd)])[None]
    qk = _proj(h, w_in, (2 * wide_seg,), WIDE_TW, 2 * wide_seg, _qk_epilogue, (qk_g,),
               (pl.BlockSpec((1, WIDE_TW), lambda i, j: (0, j)),),
               out3(BF16, 2 * d), tile_spec(WIDE_TW), "proj_qk")
    vbt = _proj(h, w_in, (4 * wide_seg,), WIDE_TW, wide_seg, _transpose_epilogue, (), (),
                jax.ShapeDtypeStruct((b, d, s), BF16),
                pl.BlockSpec((1, WIDE_TW, s), lambda i, j: (i, j, 0)), "proj_vb")
    sgb = _proj(h, w_in, (6 * wide_seg,), WIDE_TW, wide_seg, _sigmoid_epilogue, (), (),
                out3(BF16), tile_spec(WIDE_TW), "proj_gb")

    y = _mixer(qk, vbt, sgb, ug, gv, ssq, sgu_norm_g[None], w_spatial, b_spatial, lam_rows,
               subln_g[None], lambda_init)

    tm = s
    rows3 = (b * s // tm, tm, d)
    x1 = _proj(y.reshape(rows3), w_out, (0,), WIDE_TW, d // WIDE_TW, _residual_epilogue,
               (x.reshape(rows3), mod),
               (pl.BlockSpec((1, tm, WIDE_TW), lambda i, j: (i, 0, j)),
                pl.BlockSpec((1, 1, WIDE_TW), lambda i, j: ((i // (s // tm)) * 6 + 2, 0, j))),
               jax.ShapeDtypeStruct(rows3, F32),
               pl.BlockSpec((1, tm, WIDE_TW), lambda i, j: (i, 0, j)), "proj_out")
    x1 = x1.reshape(b, s, d)

    h2 = _norm_mod(x1, norm2_g, mod, scale_idx=4, shift_idx=3)
    val_off = f // PAIR_TW
    conv_b2 = conv_b[None]
    wd_rows = f // (b * val_off)
    assert wd_rows * b * val_off == f and wd_rows % (2 * SUBLANES) == 0
    wd_spec = pl.BlockSpec((wd_rows, d), lambda i, j: (i * val_off + j, 0))
    g, w_down_bf16 = _proj(
        h2, w_ff_up, (0, val_off), PAIR_TW, val_off, _conv_swiglu_epilogue,
        (conv_w, conv_w, conv_b2, conv_b2, w_ff_down),
        (pl.BlockSpec((CONV_WIDTH, PAIR_TW), lambda i, j: (0, j)),
         pl.BlockSpec((CONV_WIDTH, PAIR_TW), lambda i, j: (0, val_off + j)),
         pl.BlockSpec((1, PAIR_TW), lambda i, j: (0, j)),
         pl.BlockSpec((1, PAIR_TW), lambda i, j: (0, val_off + j)),
         wd_spec),
        (out3(BF16, f), jax.ShapeDtypeStruct((f, d), BF16)), (tile_spec(PAIR_TW), wd_spec),
        "proj_up", init=_cast_w_down)

    out = _down(g.reshape(b * s, f), w_down_bf16, x1.reshape(b * s, d), mod, gate_idx=5, seq=s)
    return out.reshape(b, s, d)


def kernel(x, c, w_ada, b_ada, norm1_g, norm2_g, w_in, sgu_norm_g, w_spatial, b_spatial, q_norm_g, k_norm_g, lambda_q1, lambda_k1, lambda_q2, lambda_k2, subln_g, w_out, w_ff_up, conv_w, conv_b, w_ff_down):
    depth = w_in.shape[0]
    c_pad = jnp.pad(c, ((0, ADA_ROWS - c.shape[0]), (0, 0)))
    for l in range(depth):
        lam_rows = jnp.stack([lambda_q1[l], lambda_k1[l], lambda_q2[l], lambda_k2[l]])
        x = _layer(x, c_pad, 0.8 - 0.6 * math.exp(-0.3 * l), w_ada[l], b_ada[l][None],
                   norm1_g[l][None], norm2_g[l][None], w_in[l], sgu_norm_g[l], w_spatial[l],
                   b_spatial[l], q_norm_g[l], k_norm_g[l], lam_rows, subln_g[l], w_out[l],
                   w_ff_up[l], conv_w[l], conv_b[l], w_ff_down[l])
    return x
```

```python
import functools
import itertools
import math

import jax
import jax.numpy as jnp
from jax import lax
from jax.experimental import pallas as pl
from jax.experimental.pallas import tpu as pltpu

F32 = jnp.float32
BF16 = jnp.bfloat16

SGU_GROUPS = 16
CHUNK = 128
DIFF_HEADS = 16
CONV_WIDTH = 3
EPS = 1e-6
NEG_INF = -1e30

LANES = 128
SUBLANES = 8
ROW_SUB = 256
K_PAD = 128
VMEM_LIMIT_BYTES = 60000 * 1024
ADA_ROWS = 16


def _params(semantics):
    return pltpu.CompilerParams(dimension_semantics=semantics, vmem_limit_bytes=VMEM_LIMIT_BYTES)


def _gelu(x):
    return 0.5 * x * (1.0 + lax.erf(x * (1.0 / math.sqrt(2.0))))


def _ada_kernel(c_ref, w_ref, b_ref, o_ref):
    c = c_ref[...]
    c_act = (c * jax.nn.sigmoid(c)).astype(BF16)
    acc = jnp.dot(c_act, w_ref[...].astype(BF16), preferred_element_type=F32)
    o_ref[...] = acc + b_ref[...]


def _ada(c_pad, w_ada, b_ada, n, tn=512):
    rows, d = c_pad.shape
    return pl.pallas_call(
        _ada_kernel,
        grid=(n // tn,),
        in_specs=[pl.BlockSpec((rows, d), lambda j: (0, 0)),
                  pl.BlockSpec((d, tn), lambda j: (0, j)),
                  pl.BlockSpec((1, tn), lambda j: (0, j))],
        out_specs=pl.BlockSpec((rows, tn), lambda j: (0, j)),
        out_shape=jax.ShapeDtypeStruct((rows, n), F32),
        compiler_params=_params(("arbitrary",)),
        name="ada",
    )(c_pad, w_ada, b_ada)


def _norm_mod_kernel(x_ref, g_ref, scale_ref, shift_ref, *refs):
    ssq_ref, o_ref = refs if len(refs) == 2 else (None, refs[0])
    x = x_ref[0]
    if ssq_ref is None:
        ms = jnp.mean(x * x, axis=-1, keepdims=True)
    else:
        ms = jnp.sum(ssq_ref[0], axis=-1, keepdims=True) * (1.0 / x.shape[1])
    y = x * lax.rsqrt(ms + EPS) * g_ref[...]
    d = x.shape[1]
    o_ref[0, :, :d] = (y * (1.0 + scale_ref[0]) + shift_ref[0]).astype(BF16)
    o_ref[0, :, d:] = jnp.zeros((x.shape[0], o_ref.shape[2] - d), BF16)


def _norm_mod(x, g, mod, pieces, scale_idx, shift_idx, ssq=None, tr=256):
    b, s, d = x.shape
    ssq_args = () if ssq is None else (ssq,)
    ssq_specs = [] if ssq is None else [pl.BlockSpec((1, tr, LANES), lambda i, r: (i, r, 0))]
    return pl.pallas_call(
        _norm_mod_kernel,
        grid=(b, s // tr),
        in_specs=[pl.BlockSpec((1, tr, d), lambda i, r: (i, r, 0)),
                  pl.BlockSpec((1, d), lambda i, r: (0, 0)),
                  pl.BlockSpec((1, 1, d), lambda i, r: (i * pieces + scale_idx, 0, 0)),
                  pl.BlockSpec((1, 1, d), lambda i, r: (i * pieces + shift_idx, 0, 0))] + ssq_specs,
        out_specs=pl.BlockSpec((1, tr, d + K_PAD), lambda i, r: (i, r, 0)),
        out_shape=jax.ShapeDtypeStruct((b, s, d + K_PAD), BF16),
        compiler_params=_params(("arbitrary", "arbitrary")),
        name="norm_mod",
    )(x, g, mod, mod, *ssq_args)


def _proj_kernel(h_hbm, *refs, n_w, tw, epilogue, init):
    cast_w = refs[0].dtype != BF16
    n_scratch = 3 if cast_w else 2
    w_refs, rest = refs[:n_w], refs[n_w:-n_scratch]
    h_ref, sems = refs[-2:]
    wbf_ref = refs[-3] if cast_w else w_refs[0]
    tm, half = h_ref.shape[0], h_ref.shape[0] // 2
    i, j = pl.program_id(0), pl.program_id(1)
    n_i, n_j = pl.num_programs(0), pl.num_programs(1)

    def h_copy(tile, part):
        src_rows = pl.ds(pl.multiple_of(tile * tm + part * half, half), half)
        return pltpu.make_async_copy(h_hbm.at[src_rows, :],
                                     h_ref.at[part * half:(part + 1) * half, :], sems.at[part])

    def body(wait_current, prefetch_next):
        if init is not None:
            init(*rest)
        if cast_w:
            for s, w_ref in enumerate(w_refs):
                wbf_ref[:, s * tw:(s + 1) * tw] = w_ref[...].astype(BF16)
        carry, r0 = None, 0
        for size in _row_splits(tm):
            for part in range(2):
                if wait_current and r0 == part * half:
                    h_copy(i, part).wait()
            rows = slice(r0, r0 + size)
            acc = jnp.dot(h_ref[rows, :wbf_ref.shape[0]], wbf_ref[...],
                          preferred_element_type=F32)
            r0 += size
            for part in range(2):
                if prefetch_next and r0 == (part + 1) * half:
                    h_copy(i + 1, part).start()
            carry = epilogue(acc, rows, carry, *rest)

    @pl.when(jnp.logical_and(i == 0, j == 0))
    def _():
        h_copy(0, 0).start()
        h_copy(0, 1).start()

    is_first = j == 0
    is_last = jnp.logical_and(j == n_j - 1, i < n_i - 1)
    pl.when(is_first)(lambda: body(True, False))
    pl.when(is_last)(lambda: body(False, True))
    pl.when(jnp.logical_not(jnp.logical_or(is_first, is_last)))(lambda: body(False, False))


def _row_splits(tm):
    return [ROW_SUB] * (tm // ROW_SUB - 1) + [ROW_SUB // 2] * 2


def _proj(h, w, w_offsets, tw, n_tiles, epilogue, extra, extra_specs, out_shape, out_specs, name,
          init=None):
    r, tm, kp = h.shape
    k = w.shape[0]
    n_w = len(w_offsets)
    splits = _row_splits(tm)
    assert n_tiles >= 2 and sum(splits) == tm and tm // 2 in itertools.accumulate(splits)
    w_specs = [pl.BlockSpec((k, tw), functools.partial(lambda i, j, off: (0, off + j), off=off))
               for off in w_offsets]
    return pl.pallas_call(
        functools.partial(_proj_kernel, n_w=n_w, tw=tw, epilogue=epilogue, init=init),
        grid=(r, n_tiles),
        in_specs=[pl.BlockSpec(memory_space=pl.ANY)] + w_specs + list(extra_specs),
        out_specs=out_specs,
        out_shape=out_shape,
        scratch_shapes=([pltpu.VMEM((k, n_w * tw), BF16)] if w.dtype != BF16 else [])
        + [pltpu.VMEM((tm, kp), BF16), pltpu.SemaphoreType.DMA((2,))],
        compiler_params=_params(("arbitrary", "arbitrary")),
        name=name,
    )(h.reshape(r * tm, kp), *([w] * n_w), *extra)


def _ug_epilogue(acc, rows, _, c_ref, wa_ref, ba_ref, o_ref, mod_ref):
    tw = acc.shape[1] // 2
    o_ref[0, rows, :] = (_gelu(acc[:, :tw]) * jax.nn.sigmoid(acc[:, tw:])).astype(o_ref.dtype)


def _ug_init(c_ref, wa_ref, ba_ref, o_ref, mod_ref):
    _ada_kernel(c_ref, wa_ref, ba_ref, mod_ref)


def _v_epilogue(acc, rows, _, o_ref, ssq_ref):
    g = _gelu(acc)
    o_ref[0, rows, :] = g.astype(o_ref.dtype)
    ssq_ref[0, rows, :] += _lane_group_sum(g * g)


def _v_init(o_ref, ssq_ref):
    @pl.when(pl.program_id(1) == 0)
    def _():
        ssq_ref[...] = jnp.zeros_like(ssq_ref)


def _qk_epilogue(acc, rows, _, g_ref, o_ref):
    for k in range(acc.shape[1] // LANES):
        cols = slice(k * LANES, (k + 1) * LANES)
        z = acc[:, cols]
        ms = jnp.mean(z * z, axis=-1, keepdims=True)
        o_ref[0, rows, cols] = (z * lax.rsqrt(ms + EPS) * g_ref[:, cols]).astype(BF16)


def _transpose_epilogue(acc, rows, _, o_ref):
    o_ref[0, :, rows] = acc.T.astype(BF16)


def _sigmoid_epilogue(acc, rows, _, o_ref):
    o_ref[0, rows, :] = jax.nn.sigmoid(acc).astype(o_ref.dtype)


def _residual_epilogue(acc, rows, _, x_ref, gate_ref, o_ref, ssq_ref=None):
    x1 = x_ref[0, rows, :] + gate_ref[0] * acc
    o_ref[0, rows, :] = x1
    if ssq_ref is not None:
        ssq_ref[0, rows, :] += _lane_group_sum(x1 * x1)


def _residual_init(x_ref, gate_ref, o_ref, ssq_ref):
    @pl.when(pl.program_id(1) == 0)
    def _():
        ssq_ref[...] = jnp.zeros_like(ssq_ref)


def _lane_group_sum(v):
    part = v[:, :LANES]
    for k in range(1, v.shape[1] // LANES):
        part = part + v[:, k * LANES:(k + 1) * LANES]
    return part


def _cast_w_down(cwg_ref, cwv_ref, cbg_ref, cbv_ref, wd_ref, o_ref, wd_bf16_ref):
    wd_bf16_ref[...] = wd_ref[...].astype(BF16)


def _conv_swiglu_epilogue(acc, rows, prev_tail, cwg_ref, cwv_ref, cbg_ref, cbv_ref, wd_ref, o_ref,
                          wd_bf16_ref):
    tw = acc.shape[1] // 2
    if prev_tail is None:
        prev_tail = jnp.zeros((SUBLANES, acc.shape[1]), F32)
    row = lax.broadcasted_iota(jnp.int32, (SUBLANES, acc.shape[1]), 0)

    def shifted(k):
        rolled = pltpu.roll(acc, k, 0)
        head = jnp.where(row >= k, rolled[:SUBLANES], pltpu.roll(prev_tail, k, 0))
        return jnp.concatenate([head, rolled[SUBLANES:]], axis=0)

    a1, a2 = shifted(1), shifted(2)

    def conv(lo, cw_ref, cb_ref):
        sl = slice(lo, lo + tw)
        return (cw_ref[0:1, :] * a2[:, sl] + cw_ref[1:2, :] * a1[:, sl]
                + cw_ref[2:3, :] * acc[:, sl] + cb_ref[...])

    gate = conv(0, cwg_ref, cbg_ref)
    val = conv(tw, cwv_ref, cbv_ref)
    o_ref[0, rows, :] = (gate * jax.nn.sigmoid(gate) * val).astype(BF16)
    return acc[-SUBLANES:]


def _spatial_gating(rows, ws, ug_ref, gv_ref, ssq_ref, sgug_ref, bs_ref):
    d_total = gv_ref.shape[2] * SGU_GROUPS
    ms = jnp.sum(ssq_ref[0, rows, :], axis=-1, keepdims=True) * (1.0 / d_total)
    vn = (gv_ref[0, rows, :] * lax.rsqrt(ms + EPS) * sgug_ref[...]).astype(BF16)
    mixed = jnp.dot(ws, vn, preferred_element_type=F32) + bs_ref[0]
    return ug_ref[0, rows, :] * mixed


def _mixer_kernel(q_ref, k_ref, vt_ref, sgb_ref, ug_ref, gv_ref, ssq_ref, sgug_ref, ws_ref, bs_ref,
                  lam_ref, subg_ref, o_ref, *, tq, hd, lambda_init):
    s_len = q_ref.shape[1]
    ws_row = lax.broadcasted_iota(jnp.int32, (CHUNK, CHUNK), 0)
    ws_col = lax.broadcasted_iota(jnp.int32, (CHUNK, CHUNK), 1)
    ws = jnp.where(ws_row >= ws_col, ws_ref[0], 0.0).astype(BF16)
    head = pl.program_id(1)
    log2e = math.log2(math.e)
    c1 = hd ** -0.5 * log2e
    slope2 = log2e * jnp.exp2(jnp.full((1, 1), -8.0 / DIFF_HEADS, F32) * (head + 1).astype(F32))
    kv_i = lax.broadcasted_iota(jnp.int32, (tq, tq), 0)
    q_i = lax.broadcasted_iota(jnp.int32, (tq, tq), 1)
    rel_bias = slope2 * (q_i - kv_i).astype(F32)
    causal = q_i >= kv_i

    lv = lam_ref[...]
    lam = (jnp.exp(jnp.sum(lv[0:1] * lv[1:2], axis=-1, keepdims=True))
           - jnp.exp(jnp.sum(lv[2:3] * lv[3:4], axis=-1, keepdims=True)) + lambda_init)

    for qi in range(s_len // tq):
        kv_len = (qi + 1) * tq
        q_rows = slice(qi * tq, (qi + 1) * tq)
        chunk_off = [slope2 * float(tq * (qi - r)) for r in range(qi + 1)]
        probs, norms = [], []
        for c in range(2):
            qc = q_ref[0, q_rows, c * hd:(c + 1) * hd]
            kc = k_ref[0, :kv_len, c * hd:(c + 1) * hd]
            s_all = lax.dot_general(kc, qc, (((1,), (1,)), ((), ())),
                                    preferred_element_type=F32)
            t, m = [], None
            for r in range(qi + 1):
                tr = s_all[r * tq:(r + 1) * tq] * c1 - rel_bias
                if r == qi:
                    tr = jnp.where(causal, tr, NEG_INF)
                mr = jnp.max(tr, axis=0, keepdims=True) - chunk_off[r]
                m = mr if m is None else jnp.maximum(m, mr)
                t.append(tr)
            p, l = [], None
            for r in range(qi + 1):
                pr = jnp.exp2(t[r] - (m + chunk_off[r]))
                lr = jnp.sum(pr, axis=0, keepdims=True)
                l = lr if l is None else l + lr
                p.append(pr)
            probs.append(p)
            norms.append(l)
        w0 = 1.0 / norms[0]
        w1 = lam / norms[1]
        a = jnp.concatenate([(probs[0][r] * w0 - probs[1][r] * w1).astype(BF16)
                             for r in range(qi + 1)], axis=0)
        o = jnp.dot(vt_ref[0, :, :kv_len], a, preferred_element_type=F32).T
        ms = jnp.mean(o * o, axis=-1, keepdims=True)
        yb = o * lax.rsqrt(ms + EPS) * subg_ref[...] * (1.0 - lambda_init)
        ya = jnp.concatenate(
            [_spatial_gating(slice(r0, r0 + CHUNK), ws, ug_ref, gv_ref, ssq_ref, sgug_ref, bs_ref)
             for r0 in range(qi * tq, (qi + 1) * tq, CHUNK)], axis=0)
        o_ref[0, q_rows, :] = (sgb_ref[0, q_rows, :] * yb + ya).astype(BF16)


def _mixer(qk, vt, sgb, ug, gv, ssq, sgu_g, w_spatial, b_spatial, lam_rows, subln_g, lambda_init,
           tq=256):
    b, d, s = vt.shape
    vd = d // DIFF_HEADS
    hd = vd // 2
    assert SGU_GROUPS == DIFF_HEADS and tq % CHUNK == 0
    head_spec = pl.BlockSpec((1, s, vd), lambda i, h: (i, 0, h))
    return pl.pallas_call(
        functools.partial(_mixer_kernel, tq=tq, hd=hd, lambda_init=lambda_init),
        grid=(b, DIFF_HEADS),
        in_specs=[head_spec,
                  pl.BlockSpec((1, s, vd), lambda i, h: (i, 0, DIFF_HEADS + h)),
                  pl.BlockSpec((1, vd, s), lambda i, h: (i, h, 0)),
                  head_spec, head_spec, head_spec,
                  pl.BlockSpec((1, s, LANES), lambda i, h: (i, 0, 0)),
                  pl.BlockSpec((1, vd), lambda i, h: (0, h)),
                  pl.BlockSpec((1, CHUNK, CHUNK), lambda i, h: (h, 0, 0)),
                  pl.BlockSpec((1, CHUNK, 1), lambda i, h: (h, 0, 0)),
                  pl.BlockSpec((4, hd), lambda i, h: (0, 0)),
                  pl.BlockSpec((1, vd), lambda i, h: (0, 0))],
        out_specs=head_spec,
        out_shape=jax.ShapeDtypeStruct((b, s, d), BF16),
        compiler_params=_params(("arbitrary", "arbitrary")),
        name="mixer",
    )(qk, qk, vt, sgb, ug, gv, ssq, sgu_g, w_spatial, b_spatial[:, :, None], lam_rows, subln_g)


MXU_TILE = 256
PAIR_TW = MXU_TILE
WIDE_TW = 2 * MXU_TILE


def _layer(x, c_pad, lambda_init, w_ada, b_ada, norm1_g, norm2_g, w_in, sgu_norm_g, w_spatial,
           b_spatial, q_norm_g, k_norm_g, lam_rows, subln_g, w_out, w_ff_up, conv_w, conv_b,
           w_ff_down):
    b, s, d = x.shape
    f = w_ff_down.shape[0]
    mod1 = _ada(c_pad, w_ada, b_ada, 2 * d)[:b].reshape(b * 2, 1, d)

    def out3(dtype, n=d):
        return jax.ShapeDtypeStruct((b, s, n), dtype)

    def tile_spec(width):
        return pl.BlockSpec((1, s, width), lambda i, j: (i, 0, j))

    h = _norm_mod(x, norm1_g, mod1, 2, scale_idx=1, shift_idx=0)

    pair_seg, wide_seg = d // PAIR_TW, d // WIDE_TW
    assert b * pair_seg * PAIR_TW == 4 * d
    ada_rows = c_pad.shape[0]
    ug, mod2 = _proj(
        h, w_in, (0 * pair_seg, 5 * pair_seg), PAIR_TW, pair_seg, _ug_epilogue,
        (c_pad, w_ada, b_ada),
        (pl.BlockSpec((ada_rows, d), lambda i, j: (0, 0)),
         pl.BlockSpec((d, PAIR_TW), lambda i, j: (0, 2 * pair_seg + i * pair_seg + j)),
         pl.BlockSpec((1, PAIR_TW), lambda i, j: (0, 2 * pair_seg + i * pair_seg + j))),
        (out3(BF16), jax.ShapeDtypeStruct((ada_rows, 4 * d), F32)),
        (tile_spec(PAIR_TW), pl.BlockSpec((ada_rows, PAIR_TW), lambda i, j: (0, i * pair_seg + j))),
        "proj_ug", init=_ug_init)
    mod2 = mod2[:b].reshape(b * 4, 1, d)
    gv, ssq = _proj(h, w_in, (1 * wide_seg,), WIDE_TW, wide_seg, _v_epilogue, (), (),
                    (out3(BF16), out3(F32, LANES)),
                    (tile_spec(WIDE_TW), pl.BlockSpec((1, s, LANES), lambda i, j: (i, 0, 0))),
                    "proj_v", init=_v_init)
    hd = q_norm_g.shape[-1]
    qk_g = jnp.concatenate([jnp.tile(q_norm_g, d // hd), jnp.tile(k_norm_g, d // hd)])[None]
    qk = _proj(h, w_in, (2 * wide_seg,), WIDE_TW, 2 * wide_seg, _qk_epilogue, (qk_g,),
               (pl.BlockSpec((1, WIDE_TW), lambda i, j: (0, j)),),
               out3(BF16, 2 * d), tile_spec(WIDE_TW), "proj_qk")
    vbt = _proj(h, w_in, (4 * wide_seg,), WIDE_TW, wide_seg, _transpose_epilogue, (), (),
                jax.ShapeDtypeStruct((b, d, s), BF16),
                pl.BlockSpec((1, WIDE_TW, s), lambda i, j: (i, j, 0)), "proj_vb")
    sgb = _proj(h, w_in, (6 * wide_seg,), WIDE_TW, wide_seg, _sigmoid_epilogue, (), (),
                out3(BF16), tile_spec(WIDE_TW), "proj_gb")

    y = _mixer(qk, vbt, sgb, ug, gv, ssq, sgu_norm_g[None], w_spatial, b_spatial, lam_rows,
               subln_g[None], lambda_init)

    tm = s
    rows3 = (b * s // tm, tm, d)
    x1, ssq1 = _proj(
        y.reshape(rows3), w_out, (0,), WIDE_TW, d // WIDE_TW, _residual_epilogue,
        (x.reshape(rows3), mod2),
        (pl.BlockSpec((1, tm, WIDE_TW), lambda i, j: (i, 0, j)),
         pl.BlockSpec((1, 1, WIDE_TW), lambda i, j: ((i // (s // tm)) * 4 + 0, 0, j))),
        (jax.ShapeDtypeStruct(rows3, F32), jax.ShapeDtypeStruct(rows3[:2] + (LANES,), F32)),
        (pl.BlockSpec((1, tm, WIDE_TW), lambda i, j: (i, 0, j)),
         pl.BlockSpec((1, tm, LANES), lambda i, j: (i, 0, 0))),
        "proj_out", init=_residual_init)
    x1 = x1.reshape(b, s, d)

    h2 = _norm_mod(x1, norm2_g, mod2, 4, scale_idx=2, shift_idx=1, ssq=ssq1.reshape(b, s, LANES))
    val_off = f // PAIR_TW
    conv_b2 = conv_b[None]
    wd_rows = f // (b * val_off)
    assert wd_rows * b * val_off == f and wd_rows % (2 * SUBLANES) == 0
    wd_spec = pl.BlockSpec((wd_rows, d), lambda i, j: (i * val_off + j, 0))
    g, w_down_bf16 = _proj(
        h2, w_ff_up, (0, val_off), PAIR_TW, val_off, _conv_swiglu_epilogue,
        (conv_w, conv_w, conv_b2, conv_b2, w_ff_down),
        (pl.BlockSpec((CONV_WIDTH, PAIR_TW), lambda i, j: (0, j)),
         pl.BlockSpec((CONV_WIDTH, PAIR_TW), lambda i, j: (0, val_off + j)),
         pl.BlockSpec((1, PAIR_TW), lambda i, j: (0, j)),
         pl.BlockSpec((1, PAIR_TW), lambda i, j: (0, val_off + j)),
         wd_spec),
        (out3(BF16, f), jax.ShapeDtypeStruct((f, d), BF16)), (tile_spec(PAIR_TW), wd_spec),
        "proj_up", init=_cast_w_down)

    tm = s // 2
    rows3 = (b * s // tm, tm, d)
    out = _proj(g.reshape(b * s // tm, tm, f), w_down_bf16, (0,), WIDE_TW, d // WIDE_TW,
                _residual_epilogue, (x1.reshape(rows3), mod2),
                (pl.BlockSpec((1, tm, WIDE_TW), lambda i, j: (i, 0, j)),
                 pl.BlockSpec((1, 1, WIDE_TW), lambda i, j: ((i // (s // tm)) * 4 + 3, 0, j))),
                jax.ShapeDtypeStruct(rows3, F32),
                pl.BlockSpec((1, tm, WIDE_TW), lambda i, j: (i, 0, j)), "proj_down")
    return out.reshape(b, s, d)


def kernel(x, c, w_ada, b_ada, norm1_g, norm2_g, w_in, sgu_norm_g, w_spatial, b_spatial, q_norm_g, k_norm_g, lambda_q1, lambda_k1, lambda_q2, lambda_k2, subln_g, w_out, w_ff_up, conv_w, conv_b, w_ff_down):
    depth = w_in.shape[0]
    c_pad = jnp.pad(c, ((0, ADA_ROWS - c.shape[0]), (0, 0)))
    for l in range(depth):
        lam_rows = jnp.stack([lambda_q1[l], lambda_k1[l], lambda_q2[l], lambda_k2[l]])
        x = _layer(x, c_pad, 0.8 - 0.6 * math.exp(-0.3 * l), w_ada[l], b_ada[l][None],
                   norm1_g[l][None], norm2_g[l][None], w_in[l], sgu_norm_g[l], w_spatial[l],
                   b_spatial[l], q_norm_g[l], k_norm_g[l], lam_rows, subln_g[l], w_out[l],
                   w_ff_up[l], conv_w[l], conv_b[l], w_ff_down[l])
    return x
```

```python
import functools
import itertools
import math

import jax
import jax.numpy as jnp
from jax import lax
from jax.experimental import pallas as pl
from jax.experimental.pallas import tpu as pltpu

F32 = jnp.float32
BF16 = jnp.bfloat16

SGU_GROUPS = 16
CHUNK = 128
DIFF_HEADS = 16
CONV_WIDTH = 3
EPS = 1e-6
NEG_INF = -1e30

LANES = 128
SUBLANES = 8
ROW_SUB = 256
K_PAD = 128
VMEM_LIMIT_BYTES = 60000 * 1024
ADA_ROWS = 16


def _params(semantics):
    return pltpu.CompilerParams(dimension_semantics=semantics, vmem_limit_bytes=VMEM_LIMIT_BYTES)


def _gelu(x):
    return 0.5 * x * (1.0 + lax.erf(x * (1.0 / math.sqrt(2.0))))


def _ada_kernel(c_ref, w_ref, b_ref, o_ref):
    c = c_ref[...]
    c_act = (c * jax.nn.sigmoid(c)).astype(BF16)
    acc = jnp.dot(c_act, w_ref[...].astype(BF16), preferred_element_type=F32)
    o_ref[...] = acc + b_ref[...]


def _ada(c_pad, w_ada, b_ada, n, tn=512):
    rows, d = c_pad.shape
    return pl.pallas_call(
        _ada_kernel,
        grid=(n // tn,),
        in_specs=[pl.BlockSpec((rows, d), lambda j: (0, 0)),
                  pl.BlockSpec((d, tn), lambda j: (0, j)),
                  pl.BlockSpec((1, tn), lambda j: (0, j))],
        out_specs=pl.BlockSpec((rows, tn), lambda j: (0, j)),
        out_shape=jax.ShapeDtypeStruct((rows, n), F32),
        compiler_params=_params(("arbitrary",)),
        name="ada",
    )(c_pad, w_ada, b_ada)


def _norm_mod_kernel(x_ref, g_ref, scale_ref, shift_ref, o_ref):
    x = x_ref[0]
    ms = jnp.mean(x * x, axis=-1, keepdims=True)
    y = x * lax.rsqrt(ms + EPS) * g_ref[...]
    d = x.shape[1]
    o_ref[0, :, :d] = (y * (1.0 + scale_ref[0]) + shift_ref[0]).astype(BF16)
    o_ref[0, :, d:] = jnp.zeros((x.shape[0], o_ref.shape[2] - d), BF16)


def _norm_mod(x, g, mod, pieces, scale_idx, shift_idx, tr=512):
    b, s, d = x.shape
    return pl.pallas_call(
        _norm_mod_kernel,
        grid=(b, s // tr),
        in_specs=[pl.BlockSpec((1, tr, d), lambda i, r: (i, r, 0)),
                  pl.BlockSpec((1, d), lambda i, r: (0, 0)),
                  pl.BlockSpec((1, 1, d), lambda i, r: (i * pieces + scale_idx, 0, 0)),
                  pl.BlockSpec((1, 1, d), lambda i, r: (i * pieces + shift_idx, 0, 0))],
        out_specs=pl.BlockSpec((1, tr, d + K_PAD), lambda i, r: (i, r, 0)),
        out_shape=jax.ShapeDtypeStruct((b, s, d + K_PAD), BF16),
        compiler_params=_params(("arbitrary", "arbitrary")),
        name="norm_mod",
    )(x, g, mod, mod)


def _proj_kernel(h_hbm, *refs, n_w, tw, epilogue, init):
    cast_w = refs[0].dtype != BF16
    n_scratch = 3 if cast_w else 2
    w_refs, rest = refs[:n_w], refs[n_w:-n_scratch]
    h_ref, sems = refs[-2:]
    wbf_ref = refs[-3] if cast_w else w_refs[0]
    tm, half = h_ref.shape[0], h_ref.shape[0] // 2
    i, j = pl.program_id(0), pl.program_id(1)
    n_i, n_j = pl.num_programs(0), pl.num_programs(1)

    def h_copy(tile, part):
        src_rows = pl.ds(pl.multiple_of(tile * tm + part * half, half), half)
        return pltpu.make_async_copy(h_hbm.at[src_rows, :],
                                     h_ref.at[part * half:(part + 1) * half, :], sems.at[part])

    def body(wait_current, prefetch_next):
        if init is not None:
            init(*rest)
        if cast_w:
            for s, w_ref in enumerate(w_refs):
                wbf_ref[:, s * tw:(s + 1) * tw] = w_ref[...].astype(BF16)
        carry, r0 = None, 0
        for size in _row_splits(tm):
            for part in range(2):
                if wait_current and r0 == part * half:
                    h_copy(i, part).wait()
            rows = slice(r0, r0 + size)
            acc = jnp.dot(h_ref[rows, :wbf_ref.shape[0]], wbf_ref[...],
                          preferred_element_type=F32)
            r0 += size
            for part in range(2):
                if prefetch_next and r0 == (part + 1) * half:
                    h_copy(i + 1, part).start()
            carry = epilogue(acc, rows, carry, *rest)

    @pl.when(jnp.logical_and(i == 0, j == 0))
    def _():
        h_copy(0, 0).start()
        h_copy(0, 1).start()

    is_first = j == 0
    is_last = jnp.logical_and(j == n_j - 1, i < n_i - 1)
    pl.when(is_first)(lambda: body(True, False))
    pl.when(is_last)(lambda: body(False, True))
    pl.when(jnp.logical_not(jnp.logical_or(is_first, is_last)))(lambda: body(False, False))


def _row_splits(tm):
    return [ROW_SUB] * (tm // ROW_SUB - 1) + [ROW_SUB // 2] * 2


def _proj(h, w, w_offsets, tw, n_tiles, epilogue, extra, extra_specs, out_shape, out_specs, name,
          init=None):
    r, tm, kp = h.shape
    k = w.shape[0]
    n_w = len(w_offsets)
    splits = _row_splits(tm)
    assert n_tiles >= 2 and sum(splits) == tm and tm // 2 in itertools.accumulate(splits)
    w_specs = [pl.BlockSpec((k, tw), functools.partial(lambda i, j, off: (0, off + j), off=off))
               for off in w_offsets]
    return pl.pallas_call(
        functools.partial(_proj_kernel, n_w=n_w, tw=tw, epilogue=epilogue, init=init),
        grid=(r, n_tiles),
        in_specs=[pl.BlockSpec(memory_space=pl.ANY)] + w_specs + list(extra_specs),
        out_specs=out_specs,
        out_shape=out_shape,
        scratch_shapes=([pltpu.VMEM((k, n_w * tw), BF16)] if w.dtype != BF16 else [])
        + [pltpu.VMEM((tm, kp), BF16), pltpu.SemaphoreType.DMA((2,))],
        compiler_params=_params(("arbitrary", "arbitrary")),
        name=name,
    )(h.reshape(r * tm, kp), *([w] * n_w), *extra)


def _ug_epilogue(acc, rows, _, c_ref, wa_ref, ba_ref, o_ref, mod_ref):
    tw = acc.shape[1] // 2
    o_ref[0, rows, :] = (_gelu(acc[:, :tw]) * jax.nn.sigmoid(acc[:, tw:])).astype(o_ref.dtype)


def _ug_init(c_ref, wa_ref, ba_ref, o_ref, mod_ref):
    _ada_kernel(c_ref, wa_ref, ba_ref, mod_ref)


def _v_epilogue(acc, rows, _, o_ref, ssq_ref):
    g = _gelu(acc)
    o_ref[0, rows, :] = g.astype(o_ref.dtype)
    ssq_ref[0, rows, :] += _lane_group_sum(g * g)


def _v_init(o_ref, ssq_ref):
    @pl.when(pl.program_id(1) == 0)
    def _():
        ssq_ref[...] = jnp.zeros_like(ssq_ref)


def _qk_epilogue(acc, rows, _, g_ref, o_ref):
    for k in range(acc.shape[1] // LANES):
        cols = slice(k * LANES, (k + 1) * LANES)
        z = acc[:, cols]
        ms = jnp.mean(z * z, axis=-1, keepdims=True)
        o_ref[0, rows, cols] = (z * lax.rsqrt(ms + EPS) * g_ref[:, cols]).astype(BF16)


def _transpose_epilogue(acc, rows, _, o_ref):
    o_ref[0, :, rows] = acc.T.astype(BF16)


def _sigmoid_epilogue(acc, rows, _, o_ref):
    o_ref[0, rows, :] = jax.nn.sigmoid(acc).astype(o_ref.dtype)


def _residual_epilogue(acc, rows, _, x_ref, gate_ref, o_ref):
    o_ref[0, rows, :] = x_ref[0, rows, :] + gate_ref[0] * acc


def _lane_group_sum(v):
    part = v[:, :LANES]
    for k in range(1, v.shape[1] // LANES):
        part = part + v[:, k * LANES:(k + 1) * LANES]
    return part


def _cast_w_down(cwg_ref, cwv_ref, cbg_ref, cbv_ref, wd_ref, o_ref, wd_bf16_ref):
    wd_bf16_ref[...] = wd_ref[...].astype(BF16)


def _conv_swiglu_epilogue(acc, rows, prev_tail, cwg_ref, cwv_ref, cbg_ref, cbv_ref, wd_ref, o_ref,
                          wd_bf16_ref):
    tw = acc.shape[1] // 2
    if prev_tail is None:
        prev_tail = jnp.zeros((SUBLANES, acc.shape[1]), F32)
    row = lax.broadcasted_iota(jnp.int32, (SUBLANES, acc.shape[1]), 0)

    def shifted(k):
        rolled = pltpu.roll(acc, k, 0)
        head = jnp.where(row >= k, rolled[:SUBLANES], pltpu.roll(prev_tail, k, 0))
        return jnp.concatenate([head, rolled[SUBLANES:]], axis=0)

    a1, a2 = shifted(1), shifted(2)

    def conv(lo, cw_ref, cb_ref):
        sl = slice(lo, lo + tw)
        return (cw_ref[0:1, :] * a2[:, sl] + cw_ref[1:2, :] * a1[:, sl]
                + cw_ref[2:3, :] * acc[:, sl] + cb_ref[...])

    gate = conv(0, cwg_ref, cbg_ref)
    val = conv(tw, cwv_ref, cbv_ref)
    o_ref[0, rows, :] = (gate * jax.nn.sigmoid(gate) * val).astype(BF16)
    return acc[-SUBLANES:]


def _spatial_gating(rows, ws, ug_ref, gv_ref, ssq_ref, sgug_ref, bs_ref):
    d_total = gv_ref.shape[2] * SGU_GROUPS
    ms = jnp.sum(ssq_ref[0, rows, :], axis=-1, keepdims=True) * (1.0 / d_total)
    vn = (gv_ref[0, rows, :] * lax.rsqrt(ms + EPS) * sgug_ref[...]).astype(BF16)
    mixed = jnp.dot(ws, vn, preferred_element_type=F32) + bs_ref[0]
    return ug_ref[0, rows, :] * mixed


def _mixer_kernel(q_ref, k_ref, vt_ref, sgb_ref, ug_ref, gv_ref, ssq_ref, sgug_ref, ws_ref, bs_ref,
                  lam_ref, subg_ref, o_ref, *, tq, hd, lambda_init):
    s_len = q_ref.shape[1]
    ws_row = lax.broadcasted_iota(jnp.int32, (CHUNK, CHUNK), 0)
    ws_col = lax.broadcasted_iota(jnp.int32, (CHUNK, CHUNK), 1)
    ws = jnp.where(ws_row >= ws_col, ws_ref[0], 0.0).astype(BF16)
    head = pl.program_id(1)
    log2e = math.log2(math.e)
    c1 = hd ** -0.5 * log2e
    slope2 = log2e * jnp.exp2(jnp.full((1, 1), -8.0 / DIFF_HEADS, F32) * (head + 1).astype(F32))
    kv_i = lax.broadcasted_iota(jnp.int32, (tq, tq), 0)
    q_i = lax.broadcasted_iota(jnp.int32, (tq, tq), 1)
    rel_bias = slope2 * (q_i - kv_i).astype(F32)
    causal = q_i >= kv_i

    lv = lam_ref[...]
    lam = (jnp.exp(jnp.sum(lv[0:1] * lv[1:2], axis=-1, keepdims=True))
           - jnp.exp(jnp.sum(lv[2:3] * lv[3:4], axis=-1, keepdims=True)) + lambda_init)

    for qi in range(s_len // tq):
        kv_len = (qi + 1) * tq
        q_rows = slice(qi * tq, (qi + 1) * tq)
        chunk_off = [slope2 * float(tq * (qi - r)) for r in range(qi + 1)]
        probs, norms = [], []
        for c in range(2):
            qc = q_ref[0, q_rows, c * hd:(c + 1) * hd]
            kc = k_ref[0, :kv_len, c * hd:(c + 1) * hd]
            s_all = lax.dot_general(kc, qc, (((1,), (1,)), ((), ())),
                                    preferred_element_type=F32)
            t, m = [], None
            for r in range(qi + 1):
                tr = s_all[r * tq:(r + 1) * tq] * c1 - rel_bias
                if r == qi:
                    tr = jnp.where(causal, tr, NEG_INF)
                mr = jnp.max(tr, axis=0, keepdims=True) - chunk_off[r]
                m = mr if m is None else jnp.maximum(m, mr)
                t.append(tr)
            p, l = [], None
            for r in range(qi + 1):
                pr = jnp.exp2(t[r] - (m + chunk_off[r]))
                lr = jnp.sum(pr, axis=0, keepdims=True)
                l = lr if l is None else l + lr
                p.append(pr)
            probs.append(p)
            norms.append(l)
        w0 = 1.0 / norms[0]
        w1 = lam / norms[1]
        a = jnp.concatenate([(probs[0][r] * w0 - probs[1][r] * w1).astype(BF16)
                             for r in range(qi + 1)], axis=0)
        o = jnp.dot(vt_ref[0, :, :kv_len], a, preferred_element_type=F32).T
        ms = jnp.mean(o * o, axis=-1, keepdims=True)
        yb = o * lax.rsqrt(ms + EPS) * subg_ref[...] * (1.0 - lambda_init)
        ya = jnp.concatenate(
            [_spatial_gating(slice(r0, r0 + CHUNK), ws, ug_ref, gv_ref, ssq_ref, sgug_ref, bs_ref)
             for r0 in range(qi * tq, (qi + 1) * tq, CHUNK)], axis=0)
        o_ref[0, q_rows, :] = (sgb_ref[0, q_rows, :] * yb + ya).astype(BF16)


def _mixer(qk, vt, sgb, ug, gv, ssq, sgu_g, w_spatial, b_spatial, lam_rows, subln_g, lambda_init,
           tq=256):
    b, d, s = vt.shape
    vd = d // DIFF_HEADS
    hd = vd // 2
    assert SGU_GROUPS == DIFF_HEADS and tq % CHUNK == 0
    head_spec = pl.BlockSpec((1, s, vd), lambda i, h: (i, 0, h))
    return pl.pallas_call(
        functools.partial(_mixer_kernel, tq=tq, hd=hd, lambda_init=lambda_init),
        grid=(b, DIFF_HEADS),
        in_specs=[head_spec,
                  pl.BlockSpec((1, s, vd), lambda i, h: (i, 0, DIFF_HEADS + h)),
                  pl.BlockSpec((1, vd, s), lambda i, h: (i, h, 0)),
                  head_spec, head_spec, head_spec,
                  pl.BlockSpec((1, s, LANES), lambda i, h: (i, 0, 0)),
                  pl.BlockSpec((1, vd), lambda i, h: (0, h)),
                  pl.BlockSpec((1, CHUNK, CHUNK), lambda i, h: (h, 0, 0)),
                  pl.BlockSpec((1, CHUNK, 1), lambda i, h: (h, 0, 0)),
                  pl.BlockSpec((4, hd), lambda i, h: (0, 0)),
                  pl.BlockSpec((1, vd), lambda i, h: (0, 0))],
        out_specs=head_spec,
        out_shape=jax.ShapeDtypeStruct((b, s, d), BF16),
        compiler_params=_params(("arbitrary", "arbitrary")),
        name="mixer",
    )(qk, qk, vt, sgb, ug, gv, ssq, sgu_g, w_spatial, b_spatial[:, :, None], lam_rows, subln_g)


MXU_TILE = 256
PAIR_TW = MXU_TILE
WIDE_TW = 2 * MXU_TILE


def _layer(x, c_pad, lambda_init, w_ada, b_ada, norm1_g, norm2_g, w_in, sgu_norm_g, w_spatial,
           b_spatial, q_norm_g, k_norm_g, lam_rows, subln_g, w_out, w_ff_up, conv_w, conv_b,
           w_ff_down):
    b, s, d = x.shape
    f = w_ff_down.shape[0]
    mod1 = _ada(c_pad, w_ada, b_ada, 2 * d)[:b].reshape(b * 2, 1, d)

    def out3(dtype, n=d):
        return jax.ShapeDtypeStruct((b, s, n), dtype)

    def tile_spec(width):
        return pl.BlockSpec((1, s, width), lambda i, j: (i, 0, j))

    h = _norm_mod(x, norm1_g, mod1, 2, scale_idx=1, shift_idx=0)

    pair_seg, wide_seg = d // PAIR_TW, d // WIDE_TW
    assert b * pair_seg * PAIR_TW == 4 * d
    ada_rows = c_pad.shape[0]
    ug, mod2 = _proj(
        h, w_in, (0 * pair_seg, 5 * pair_seg), PAIR_TW, pair_seg, _ug_epilogue,
        (c_pad, w_ada, b_ada),
        (pl.BlockSpec((ada_rows, d), lambda i, j: (0, 0)),
         pl.BlockSpec((d, PAIR_TW), lambda i, j: (0, 2 * pair_seg + i * pair_seg + j)),
         pl.BlockSpec((1, PAIR_TW), lambda i, j: (0, 2 * pair_seg + i * pair_seg + j))),
        (out3(BF16), jax.ShapeDtypeStruct((ada_rows, 4 * d), F32)),
        (tile_spec(PAIR_TW), pl.BlockSpec((ada_rows, PAIR_TW), lambda i, j: (0, i * pair_seg + j))),
        "proj_ug", init=_ug_init)
    mod2 = mod2[:b].reshape(b * 4, 1, d)
    gv, ssq = _proj(h, w_in, (1 * wide_seg,), WIDE_TW, wide_seg, _v_epilogue, (), (),
                    (out3(BF16), out3(F32, LANES)),
                    (tile_spec(WIDE_TW), pl.BlockSpec((1, s, LANES), lambda i, j: (i, 0, 0))),
                    "proj_v", init=_v_init)
    hd = q_norm_g.shape[-1]
    qk_g = jnp.concatenate([jnp.tile(q_norm_g, d // hd), jnp.tile(k_norm_g, d // hd)])[None]
    qk = _proj(h, w_in, (2 * wide_seg,), WIDE_TW, 2 * wide_seg, _qk_epilogue, (qk_g,),
               (pl.BlockSpec((1, WIDE_TW), lambda i, j: (0, j)),),
               out3(BF16, 2 * d), tile_spec(WIDE_TW), "proj_qk")
    vbt = _proj(h, w_in, (4 * wide_seg,), WIDE_TW, wide_seg, _transpose_epilogue, (), (),
                jax.ShapeDtypeStruct((b, d, s), BF16),
                pl.BlockSpec((1, WIDE_TW, s), lambda i, j: (i, j, 0)), "proj_vb")
    sgb = _proj(h, w_in, (6 * wide_seg,), WIDE_TW, wide_seg, _sigmoid_epilogue, (), (),
                out3(BF16), tile_spec(WIDE_TW), "proj_gb")

    y = _mixer(qk, vbt, sgb, ug, gv, ssq, sgu_norm_g[None], w_spatial, b_spatial, lam_rows,
               subln_g[None], lambda_init)

    tm = s
    rows3 = (b * s // tm, tm, d)
    x1 = _proj(y.reshape(rows3), w_out, (0,), WIDE_TW, d // WIDE_TW, _residual_epilogue,
               (x.reshape(rows3), mod2),
               (pl.BlockSpec((1, tm, WIDE_TW), lambda i, j: (i, 0, j)),
                pl.BlockSpec((1, 1, WIDE_TW), lambda i, j: ((i // (s // tm)) * 4 + 0, 0, j))),
               jax.ShapeDtypeStruct(rows3, F32),
               pl.BlockSpec((1, tm, WIDE_TW), lambda i, j: (i, 0, j)), "proj_out")
    x1 = x1.reshape(b, s, d)

    h2 = _norm_mod(x1, norm2_g, mod2, 4, scale_idx=2, shift_idx=1)
    val_off = f // PAIR_TW
    conv_b2 = conv_b[None]
    wd_rows = f // (b * val_off)
    assert wd_rows * b * val_off == f and wd_rows % (2 * SUBLANES) == 0
    wd_spec = pl.BlockSpec((wd_rows, d), lambda i, j: (i * val_off + j, 0))
    g, w_down_bf16 = _proj(
        h2, w_ff_up, (0, val_off), PAIR_TW, val_off, _conv_swiglu_epilogue,
        (conv_w, conv_w, conv_b2, conv_b2, w_ff_down),
        (pl.BlockSpec((CONV_WIDTH, PAIR_TW), lambda i, j: (0, j)),
         pl.BlockSpec((CONV_WIDTH, PAIR_TW), lambda i, j: (0, val_off + j)),
         pl.BlockSpec((1, PAIR_TW), lambda i, j: (0, j)),
         pl.BlockSpec((1, PAIR_TW), lambda i, j: (0, val_off + j)),
         wd_spec),
        (out3(BF16, f), jax.ShapeDtypeStruct((f, d), BF16)), (tile_spec(PAIR_TW), wd_spec),
        "proj_up", init=_cast_w_down)

    tm = s // 2
    rows3 = (b * s // tm, tm, d)
    out = _proj(g.reshape(b * s // tm, tm, f), w_down_bf16, (0,), WIDE_TW, d // WIDE_TW,
                _residual_epilogue, (x1.reshape(rows3), mod2),
                (pl.BlockSpec((1, tm, WIDE_TW), lambda i, j: (i, 0, j)),
                 pl.BlockSpec((1, 1, WIDE_TW), lambda i, j: ((i // (s // tm)) * 4 + 3, 0, j))),
                jax.ShapeDtypeStruct(rows3, F32),
                pl.BlockSpec((1, tm, WIDE_TW), lambda i, j: (i, 0, j)), "proj_down")
    return out.reshape(b, s, d)


def kernel(x, c, w_ada, b_ada, norm1_g, norm2_g, w_in, sgu_norm_g, w_spatial, b_spatial, q_norm_g, k_norm_g, lambda_q1, lambda_k1, lambda_q2, lambda_k2, subln_g, w_out, w_ff_up, conv_w, conv_b, w_ff_down):
    depth = w_in.shape[0]
    c_pad = jnp.pad(c, ((0, ADA_ROWS - c.shape[0]), (0, 0)))
    for l in range(depth):
        lam_rows = jnp.stack([lambda_q1[l], lambda_k1[l], lambda_q2[l], lambda_k2[l]])
        x = _layer(x, c_pad, 0.8 - 0.6 * math.exp(-0.3 * l), w_ada[l], b_ada[l][None],
                   norm1_g[l][None], norm2_g[l][None], w_in[l], sgu_norm_g[l], w_spatial[l],
                   b_spatial[l], q_norm_g[l], k_norm_g[l], lam_rows, subln_g[l], w_out[l],
                   w_ff_up[l], conv_w[l], conv_b[l], w_ff_down[l])
    return x
```

```python
import functools
import itertools
import math

import jax
import jax.numpy as jnp
from jax import lax
from jax.experimental import pallas as pl
from jax.experimental.pallas import tpu as pltpu

F32 = jnp.float32
BF16 = jnp.bfloat16

SGU_GROUPS = 16
CHUNK = 128
DIFF_HEADS = 16
CONV_WIDTH = 3
EPS = 1e-6
NEG_INF = -1e30

LANES = 128
SUBLANES = 8
ROW_SUB = 256
K_PAD = 128
VMEM_LIMIT_BYTES = 60000 * 1024
ADA_ROWS = 16


def _params(semantics):
    return pltpu.CompilerParams(dimension_semantics=semantics, vmem_limit_bytes=VMEM_LIMIT_BYTES)


def _gelu(x):
    return 0.5 * x * (1.0 + lax.erf(x * (1.0 / math.sqrt(2.0))))


def _ada_kernel(c_ref, w_ref, b_ref, o_ref):
    c = c_ref[...]
    c_act = (c * jax.nn.sigmoid(c)).astype(BF16)
    acc = jnp.dot(c_act, w_ref[...].astype(BF16), preferred_element_type=F32)
    o_ref[...] = acc + b_ref[...]


def _ada(c_pad, w_ada, b_ada, n, tn=512):
    rows, d = c_pad.shape
    return pl.pallas_call(
        _ada_kernel,
        grid=(n // tn,),
        in_specs=[pl.BlockSpec((rows, d), lambda j: (0, 0)),
                  pl.BlockSpec((d, tn), lambda j: (0, j)),
                  pl.BlockSpec((1, tn), lambda j: (0, j))],
        out_specs=pl.BlockSpec((rows, tn), lambda j: (0, j)),
        out_shape=jax.ShapeDtypeStruct((rows, n), F32),
        compiler_params=_params(("arbitrary",)),
        name="ada",
    )(c_pad, w_ada, b_ada)


def _norm_mod_kernel(x_ref, g_ref, scale_ref, shift_ref, o_ref):
    x = x_ref[0]
    ms = jnp.mean(x * x, axis=-1, keepdims=True)
    y = x * lax.rsqrt(ms + EPS) * g_ref[...]
    d = x.shape[1]
    o_ref[0, :, :d] = (y * (1.0 + scale_ref[0]) + shift_ref[0]).astype(BF16)
    o_ref[0, :, d:] = jnp.zeros((x.shape[0], o_ref.shape[2] - d), BF16)


def _norm_mod(x, g, mod, pieces, scale_idx, shift_idx, tr=512):
    b, s, d = x.shape
    return pl.pallas_call(
        _norm_mod_kernel,
        grid=(b, s // tr),
        in_specs=[pl.BlockSpec((1, tr, d), lambda i, r: (i, r, 0)),
                  pl.BlockSpec((1, d), lambda i, r: (0, 0)),
                  pl.BlockSpec((1, 1, d), lambda i, r: (i * pieces + scale_idx, 0, 0)),
                  pl.BlockSpec((1, 1, d), lambda i, r: (i * pieces + shift_idx, 0, 0))],
        out_specs=pl.BlockSpec((1, tr, d + K_PAD), lambda i, r: (i, r, 0)),
        out_shape=jax.ShapeDtypeStruct((b, s, d + K_PAD), BF16),
        compiler_params=_params(("arbitrary", "arbitrary")),
        name="norm_mod",
    )(x, g, mod, mod)


def _proj_kernel(h_hbm, *refs, n_w, tw, epilogue, init):
    cast_w = refs[0].dtype != BF16
    n_scratch = 3 if cast_w else 2
    w_refs, rest = refs[:n_w], refs[n_w:-n_scratch]
    h_ref, sems = refs[-2:]
    wbf_ref = refs[-3] if cast_w else w_refs[0]
    tm, half = h_ref.shape[0], h_ref.shape[0] // 2
    i, j = pl.program_id(0), pl.program_id(1)
    n_i, n_j = pl.num_programs(0), pl.num_programs(1)

    def h_copy(tile, part):
        src_rows = pl.ds(pl.multiple_of(tile * tm + part * half, half), half)
        return pltpu.make_async_copy(h_hbm.at[src_rows, :],
                                     h_ref.at[part * half:(part + 1) * half, :], sems.at[part])

    def body(wait_current, prefetch_next):
        if init is not None:
            init(*rest)
        if cast_w:
            for s, w_ref in enumerate(w_refs):
                wbf_ref[:, s * tw:(s + 1) * tw] = w_ref[...].astype(BF16)
        carry, r0 = None, 0
        for size in _row_splits(tm):
            for part in range(2):
                if wait_current and r0 == part * half:
                    h_copy(i, part).wait()
            rows = slice(r0, r0 + size)
            acc = jnp.dot(h_ref[rows, :wbf_ref.shape[0]], wbf_ref[...],
                          preferred_element_type=F32)
            r0 += size
            for part in range(2):
                if prefetch_next and r0 == (part + 1) * half:
                    h_copy(i + 1, part).start()
            carry = epilogue(acc, rows, carry, *rest)

    @pl.when(jnp.logical_and(i == 0, j == 0))
    def _():
        h_copy(0, 0).start()
        h_copy(0, 1).start()

    is_first = j == 0
    is_last = jnp.logical_and(j == n_j - 1, i < n_i - 1)
    pl.when(is_first)(lambda: body(True, False))
    pl.when(is_last)(lambda: body(False, True))
    pl.when(jnp.logical_not(jnp.logical_or(is_first, is_last)))(lambda: body(False, False))


def _row_splits(tm):
    return [ROW_SUB] * (tm // ROW_SUB - 1) + [ROW_SUB // 2] * 2


def _proj(h, w, w_offsets, tw, n_tiles, epilogue, extra, extra_specs, out_shape, out_specs, name,
          init=None):
    r, tm, kp = h.shape
    k = w.shape[0]
    n_w = len(w_offsets)
    splits = _row_splits(tm)
    assert n_tiles >= 2 and sum(splits) == tm and tm // 2 in itertools.accumulate(splits)
    w_specs = [pl.BlockSpec((k, tw), functools.partial(lambda i, j, off: (0, off + j), off=off))
               for off in w_offsets]
    return pl.pallas_call(
        functools.partial(_proj_kernel, n_w=n_w, tw=tw, epilogue=epilogue, init=init),
        grid=(r, n_tiles),
        in_specs=[pl.BlockSpec(memory_space=pl.ANY)] + w_specs + list(extra_specs),
        out_specs=out_specs,
        out_shape=out_shape,
        scratch_shapes=([pltpu.VMEM((k, n_w * tw), BF16)] if w.dtype != BF16 else [])
        + [pltpu.VMEM((tm, kp), BF16), pltpu.SemaphoreType.DMA((2,))],
        compiler_params=_params(("arbitrary", "arbitrary")),
        name=name,
    )(h.reshape(r * tm, kp), *([w] * n_w), *extra)


def _ug_epilogue(acc, rows, _, c_ref, wa_ref, ba_ref, o_ref, mod_ref):
    tw = acc.shape[1] // 2
    o_ref[0, rows, :] = (_gelu(acc[:, :tw]) * jax.nn.sigmoid(acc[:, tw:])).astype(o_ref.dtype)


def _ug_init(c_ref, wa_ref, ba_ref, o_ref, mod_ref):
    _ada_kernel(c_ref, wa_ref, ba_ref, mod_ref)


def _v_epilogue(acc, rows, _, o_ref, ssq_ref):
    g = _gelu(acc)
    o_ref[0, rows, :] = g.astype(o_ref.dtype)
    ssq_ref[0, rows, :] += _lane_group_sum(g * g)


def _v_init(o_ref, ssq_ref):
    @pl.when(pl.program_id(1) == 0)
    def _():
        ssq_ref[...] = jnp.zeros_like(ssq_ref)


def _qk_epilogue(acc, rows, _, g_ref, o_ref):
    for k in range(acc.shape[1] // LANES):
        cols = slice(k * LANES, (k + 1) * LANES)
        z = acc[:, cols]
        ms = jnp.mean(z * z, axis=-1, keepdims=True)
        o_ref[0, rows, cols] = (z * lax.rsqrt(ms + EPS) * g_ref[:, cols]).astype(BF16)


def _transpose_epilogue(acc, rows, _, o_ref):
    o_ref[0, :, rows] = acc.T.astype(BF16)


def _sigmoid_epilogue(acc, rows, _, o_ref):
    o_ref[0, rows, :] = jax.nn.sigmoid(acc).astype(o_ref.dtype)


def _residual_epilogue(acc, rows, _, x_ref, gate_ref, o_ref):
    o_ref[0, rows, :] = x_ref[0, rows, :] + gate_ref[0] * acc


def _lane_group_sum(v):
    part = v[:, :LANES]
    for k in range(1, v.shape[1] // LANES):
        part = part + v[:, k * LANES:(k + 1) * LANES]
    return part


def _cast_w_down(cwg_ref, cwv_ref, cbg_ref, cbv_ref, wd_ref, o_ref, wd_bf16_ref):
    wd_bf16_ref[...] = wd_ref[...].astype(BF16)


def _conv_swiglu_epilogue(acc, rows, prev_tail, cwg_ref, cwv_ref, cbg_ref, cbv_ref, wd_ref, o_ref,
                          wd_bf16_ref):
    tw = acc.shape[1] // 2
    if prev_tail is None:
        prev_tail = jnp.zeros((SUBLANES, acc.shape[1]), F32)
    row = lax.broadcasted_iota(jnp.int32, (SUBLANES, acc.shape[1]), 0)

    def shifted(k):
        rolled = pltpu.roll(acc, k, 0)
        head = jnp.where(row >= k, rolled[:SUBLANES], pltpu.roll(prev_tail, k, 0))
        return jnp.concatenate([head, rolled[SUBLANES:]], axis=0)

    a1, a2 = shifted(1), shifted(2)

    def conv(lo, cw_ref, cb_ref):
        sl = slice(lo, lo + tw)
        return (cw_ref[0:1, :] * a2[:, sl] + cw_ref[1:2, :] * a1[:, sl]
                + cw_ref[2:3, :] * acc[:, sl] + cb_ref[...])

    gate = conv(0, cwg_ref, cbg_ref)
    val = conv(tw, cwv_ref, cbv_ref)
    o_ref[0, rows, :] = (gate * jax.nn.sigmoid(gate) * val).astype(BF16)
    return acc[-SUBLANES:]


def _spatial_gating(rows, ws, ug_ref, gv_ref, ssq_ref, sgug_ref, bs_ref):
    d_total = gv_ref.shape[2] * SGU_GROUPS
    ms = jnp.sum(ssq_ref[0, rows, :], axis=-1, keepdims=True) * (1.0 / d_total)
    vn = (gv_ref[0, rows, :] * lax.rsqrt(ms + EPS) * sgug_ref[...]).astype(BF16)
    mixed = jnp.dot(ws, vn, preferred_element_type=F32) + bs_ref[0]
    return ug_ref[0, rows, :] * mixed


def _mixer_kernel(q_ref, k_ref, vt_ref, sgb_ref, ug_ref, gv_ref, ssq_ref, sgug_ref, ws_ref, bs_ref,
                  lam_ref, subg_ref, o_ref, *, tq, hd, lambda_init):
    s_len = q_ref.shape[1]
    ws_row = lax.broadcasted_iota(jnp.int32, (CHUNK, CHUNK), 0)
    ws_col = lax.broadcasted_iota(jnp.int32, (CHUNK, CHUNK), 1)
    ws = jnp.where(ws_row >= ws_col, ws_ref[0], 0.0).astype(BF16)
    head = pl.program_id(1)
    log2e = math.log2(math.e)
    c1 = hd ** -0.5 * log2e
    slope2 = log2e * jnp.exp2(jnp.full((1, 1), -8.0 / DIFF_HEADS, F32) * (head + 1).astype(F32))
    kv_i = lax.broadcasted_iota(jnp.int32, (tq, tq), 0)
    q_i = lax.broadcasted_iota(jnp.int32, (tq, tq), 1)
    rel_bias = slope2 * (q_i - kv_i).astype(F32)
    causal = q_i >= kv_i

    lv = lam_ref[...]
    lam = (jnp.exp(jnp.sum(lv[0:1] * lv[1:2], axis=-1, keepdims=True))
           - jnp.exp(jnp.sum(lv[2:3] * lv[3:4], axis=-1, keepdims=True)) + lambda_init)

    for qi in range(s_len // tq):
        kv_len = (qi + 1) * tq
        q_rows = slice(qi * tq, (qi + 1) * tq)
        chunk_off = [slope2 * float(tq * (qi - r)) for r in range(qi + 1)]
        outs = []
        for c in range(2):
            qc = q_ref[0, q_rows, c * hd:(c + 1) * hd]
            kc = k_ref[0, :kv_len, c * hd:(c + 1) * hd]
            s_all = lax.dot_general(kc, qc, (((1,), (1,)), ((), ())),
                                    preferred_element_type=F32)
            t, m = [], None
            for r in range(qi + 1):
                tr = s_all[r * tq:(r + 1) * tq] * c1 - rel_bias
                if r == qi:
                    tr = jnp.where(causal, tr, NEG_INF)
                mr = jnp.max(tr, axis=0, keepdims=True) - chunk_off[r]
                m = mr if m is None else jnp.maximum(m, mr)
                t.append(tr)
            p, l = [], None
            for r in range(qi + 1):
                pr = jnp.exp2(t[r] - (m + chunk_off[r]))
                lr = jnp.sum(pr, axis=0, keepdims=True)
                l = lr if l is None else l + lr
                p.append(pr.astype(BF16))
            pv = jnp.dot(vt_ref[0, :, :kv_len], jnp.concatenate(p, axis=0),
                         preferred_element_type=F32)
            outs.append(pv * ((1.0 if c == 0 else lam) / l))
        o = (outs[0] - outs[1]).T
        ms = jnp.mean(o * o, axis=-1, keepdims=True)
        yb = o * lax.rsqrt(ms + EPS) * subg_ref[...] * (1.0 - lambda_init)
        ya = jnp.concatenate(
            [_spatial_gating(slice(r0, r0 + CHUNK), ws, ug_ref, gv_ref, ssq_ref, sgug_ref, bs_ref)
             for r0 in range(qi * tq, (qi + 1) * tq, CHUNK)], axis=0)
        o_ref[0, q_rows, :] = (sgb_ref[0, q_rows, :] * yb + ya).astype(BF16)


def _mixer(qk, vt, sgb, ug, gv, ssq, sgu_g, w_spatial, b_spatial, lam_rows, subln_g, lambda_init,
           tq=256):
    b, d, s = vt.shape
    vd = d // DIFF_HEADS
    hd = vd // 2
    assert SGU_GROUPS == DIFF_HEADS and tq % CHUNK == 0
    head_spec = pl.BlockSpec((1, s, vd), lambda i, h: (i, 0, h))
    return pl.pallas_call(
        functools.partial(_mixer_kernel, tq=tq, hd=hd, lambda_init=lambda_init),
        grid=(b, DIFF_HEADS),
        in_specs=[head_spec,
                  pl.BlockSpec((1, s, vd), lambda i, h: (i, 0, DIFF_HEADS + h)),
                  pl.BlockSpec((1, vd, s), lambda i, h: (i, h, 0)),
                  head_spec, head_spec, head_spec,
                  pl.BlockSpec((1, s, LANES), lambda i, h: (i, 0, 0)),
                  pl.BlockSpec((1, vd), lambda i, h: (0, h)),
                  pl.BlockSpec((1, CHUNK, CHUNK), lambda i, h: (h, 0, 0)),
                  pl.BlockSpec((1, CHUNK, 1), lambda i, h: (h, 0, 0)),
                  pl.BlockSpec((4, hd), lambda i, h: (0, 0)),
                  pl.BlockSpec((1, vd), lambda i, h: (0, 0))],
        out_specs=head_spec,
        out_shape=jax.ShapeDtypeStruct((b, s, d), BF16),
        compiler_params=_params(("arbitrary", "arbitrary")),
        name="mixer",
    )(qk, qk, vt, sgb, ug, gv, ssq, sgu_g, w_spatial, b_spatial[:, :, None], lam_rows, subln_g)


MXU_TILE = 256
PAIR_TW = MXU_TILE
WIDE_TW = 2 * MXU_TILE


def _layer(x, c_pad, lambda_init, w_ada, b_ada, norm1_g, norm2_g, w_in, sgu_norm_g, w_spatial,
           b_spatial, q_norm_g, k_norm_g, lam_rows, subln_g, w_out, w_ff_up, conv_w, conv_b,
           w_ff_down):
    b, s, d = x.shape
    f = w_ff_down.shape[0]
    mod1 = _ada(c_pad, w_ada, b_ada, 2 * d)[:b].reshape(b * 2, 1, d)

    def out3(dtype, n=d):
        return jax.ShapeDtypeStruct((b, s, n), dtype)

    def tile_spec(width):
        return pl.BlockSpec((1, s, width), lambda i, j: (i, 0, j))

    h = _norm_mod(x, norm1_g, mod1, 2, scale_idx=1, shift_idx=0)

    pair_seg, wide_seg = d // PAIR_TW, d // WIDE_TW
    assert b * pair_seg * PAIR_TW == 4 * d
    ada_rows = c_pad.shape[0]
    ug, mod2 = _proj(
        h, w_in, (0 * pair_seg, 5 * pair_seg), PAIR_TW, pair_seg, _ug_epilogue,
        (c_pad, w_ada, b_ada),
        (pl.BlockSpec((ada_rows, d), lambda i, j: (0, 0)),
         pl.BlockSpec((d, PAIR_TW), lambda i, j: (0, 2 * pair_seg + i * pair_seg + j)),
         pl.BlockSpec((1, PAIR_TW), lambda i, j: (0, 2 * pair_seg + i * pair_seg + j))),
        (out3(BF16), jax.ShapeDtypeStruct((ada_rows, 4 * d), F32)),
        (tile_spec(PAIR_TW), pl.BlockSpec((ada_rows, PAIR_TW), lambda i, j: (0, i * pair_seg + j))),
        "proj_ug", init=_ug_init)
    mod2 = mod2[:b].reshape(b * 4, 1, d)
    gv, ssq = _proj(h, w_in, (1 * wide_seg,), WIDE_TW, wide_seg, _v_epilogue, (), (),
                    (out3(BF16), out3(F32, LANES)),
                    (tile_spec(WIDE_TW), pl.BlockSpec((1, s, LANES), lambda i, j: (i, 0, 0))),
                    "proj_v", init=_v_init)
    hd = q_norm_g.shape[-1]
    qk_g = jnp.concatenate([jnp.tile(q_norm_g, d // hd), jnp.tile(k_norm_g, d // hd)])[None]
    qk = _proj(h, w_in, (2 * wide_seg,), WIDE_TW, 2 * wide_seg, _qk_epilogue, (qk_g,),
               (pl.BlockSpec((1, WIDE_TW), lambda i, j: (0, j)),),
               out3(BF16, 2 * d), tile_spec(WIDE_TW), "proj_qk")
    vbt = _proj(h, w_in, (4 * wide_seg,), WIDE_TW, wide_seg, _transpose_epilogue, (), (),
                jax.ShapeDtypeStruct((b, d, s), BF16),
                pl.BlockSpec((1, WIDE_TW, s), lambda i, j: (i, j, 0)), "proj_vb")
    sgb = _proj(h, w_in, (6 * wide_seg,), WIDE_TW, wide_seg, _sigmoid_epilogue, (), (),
                out3(BF16), tile_spec(WIDE_TW), "proj_gb")

    y = _mixer(qk, vbt, sgb, ug, gv, ssq, sgu_norm_g[None], w_spatial, b_spatial, lam_rows,
               subln_g[None], lambda_init)

    tm = s
    rows3 = (b * s // tm, tm, d)
    x1 = _proj(y.reshape(rows3), w_out, (0,), WIDE_TW, d // WIDE_TW, _residual_epilogue,
               (x.reshape(rows3), mod2),
               (pl.BlockSpec((1, tm, WIDE_TW), lambda i, j: (i, 0, j)),
                pl.BlockSpec((1, 1, WIDE_TW), lambda i, j: ((i // (s // tm)) * 4 + 0, 0, j))),
               jax.ShapeDtypeStruct(rows3, F32),
               pl.BlockSpec((1, tm, WIDE_TW), lambda i, j: (i, 0, j)), "proj_out")
    x1 = x1.reshape(b, s, d)

    h2 = _norm_mod(x1, norm2_g, mod2, 4, scale_idx=2, shift_idx=1)
    val_off = f // PAIR_TW
    conv_b2 = conv_b[None]
    wd_rows = f // (b * val_off)
    assert wd_rows * b * val_off == f and wd_rows % (2 * SUBLANES) == 0
    wd_spec = pl.BlockSpec((wd_rows, d), lambda i, j: (i * val_off + j, 0))
    g, w_down_bf16 = _proj(
        h2, w_ff_up, (0, val_off), PAIR_TW, val_off, _conv_swiglu_epilogue,
        (conv_w, conv_w, conv_b2, conv_b2, w_ff_down),
        (pl.BlockSpec((CONV_WIDTH, PAIR_TW), lambda i, j: (0, j)),
         pl.BlockSpec((CONV_WIDTH, PAIR_TW), lambda i, j: (0, val_off + j)),
         pl.BlockSpec((1, PAIR_TW), lambda i, j: (0, j)),
         pl.BlockSpec((1, PAIR_TW), lambda i, j: (0, val_off + j)),
         wd_spec),
        (out3(BF16, f), jax.ShapeDtypeStruct((f, d), BF16)), (tile_spec(PAIR_TW), wd_spec),
        "proj_up", init=_cast_w_down)

    tm = s // 2
    rows3 = (b * s // tm, tm, d)
    out = _proj(g.reshape(b * s // tm, tm, f), w_down_bf16, (0,), WIDE_TW, d // WIDE_TW,
                _residual_epilogue, (x1.reshape(rows3), mod2),
                (pl.BlockSpec((1, tm, WIDE_TW), lambda i, j: (i, 0, j)),
                 pl.BlockSpec((1, 1, WIDE_TW), lambda i, j: ((i // (s // tm)) * 4 + 3, 0, j))),
                jax.ShapeDtypeStruct(rows3, F32),
                pl.BlockSpec((1, tm, WIDE_TW), lambda i, j: (i, 0, j)), "proj_down")
    return out.reshape(b, s, d)


def kernel(x, c, w_ada, b_ada, norm1_g, norm2_g, w_in, sgu_norm_g, w_spatial, b_spatial, q_norm_g, k_norm_g, lambda_q1, lambda_k1, lambda_q2, lambda_k2, subln_g, w_out, w_ff_up, conv_w, conv_b, w_ff_down):
    depth = w_in.shape[0]
    c_pad = jnp.pad(c, ((0, ADA_ROWS - c.shape[0]), (0, 0)))
    for l in range(depth):
        lam_rows = jnp.stack([lambda_q1[l], lambda_k1[l], lambda_q2[l], lambda_k2[l]])
        x = _layer(x, c_pad, 0.8 - 0.6 * math.exp(-0.3 * l), w_ada[l], b_ada[l][None],
                   norm1_g[l][None], norm2_g[l][None], w_in[l], sgu_norm_g[l], w_spatial[l],
                   b_spatial[l], q_norm_g[l], k_norm_g[l], lam_rows, subln_g[l], w_out[l],
                   w_ff_up[l], conv_w[l], conv_b[l], w_ff_down[l])
    return x
```

```python
import functools
import itertools
import math

import jax
import jax.numpy as jnp
from jax import lax
from jax.experimental import pallas as pl
from jax.experimental.pallas import tpu as pltpu

F32 = jnp.float32
BF16 = jnp.bfloat16

SGU_GROUPS = 16
CHUNK = 128
DIFF_HEADS = 16
CONV_WIDTH = 3
EPS = 1e-6
NEG_INF = -1e30

LANES = 128
SUBLANES = 8
ROW_SUB = 256
K_PAD = 128
VMEM_LIMIT_BYTES = 60000 * 1024
ADA_ROWS = 16


def _params(semantics):
    return pltpu.CompilerParams(dimension_semantics=semantics, vmem_limit_bytes=VMEM_LIMIT_BYTES)


def _gelu(x):
    return 0.5 * x * (1.0 + lax.erf(x * (1.0 / math.sqrt(2.0))))


def _ada_kernel(c_ref, w_ref, b_ref, o_ref):
    c = c_ref[...]
    c_act = (c * jax.nn.sigmoid(c)).astype(BF16)
    acc = jnp.dot(c_act, w_ref[...].astype(BF16), preferred_element_type=F32)
    o_ref[...] = acc + b_ref[...]


def _ada(c_pad, w_ada, b_ada, n, tn=512):
    rows, d = c_pad.shape
    return pl.pallas_call(
        _ada_kernel,
        grid=(n // tn,),
        in_specs=[pl.BlockSpec((rows, d), lambda j: (0, 0)),
                  pl.BlockSpec((d, tn), lambda j: (0, j)),
                  pl.BlockSpec((1, tn), lambda j: (0, j))],
        out_specs=pl.BlockSpec((rows, tn), lambda j: (0, j)),
        out_shape=jax.ShapeDtypeStruct((rows, n), F32),
        compiler_params=_params(("arbitrary",)),
        name="ada",
    )(c_pad, w_ada, b_ada)


def _norm_mod_kernel(x_ref, g_ref, scale_ref, shift_ref, o_ref):
    x = x_ref[0]
    ms = jnp.mean(x * x, axis=-1, keepdims=True)
    y = x * lax.rsqrt(ms + EPS) * g_ref[...]
    d = x.shape[1]
    o_ref[0, :, :d] = (y * (1.0 + scale_ref[0]) + shift_ref[0]).astype(BF16)
    o_ref[0, :, d:] = jnp.zeros((x.shape[0], o_ref.shape[2] - d), BF16)


def _norm_mod(x, g, mod, pieces, scale_idx, shift_idx, tr=512):
    b, s, d = x.shape
    return pl.pallas_call(
        _norm_mod_kernel,
        grid=(b, s // tr),
        in_specs=[pl.BlockSpec((1, tr, d), lambda i, r: (i, r, 0)),
                  pl.BlockSpec((1, d), lambda i, r: (0, 0)),
                  pl.BlockSpec((1, 1, d), lambda i, r: (i * pieces + scale_idx, 0, 0)),
                  pl.BlockSpec((1, 1, d), lambda i, r: (i * pieces + shift_idx, 0, 0))],
        out_specs=pl.BlockSpec((1, tr, d + K_PAD), lambda i, r: (i, r, 0)),
        out_shape=jax.ShapeDtypeStruct((b, s, d + K_PAD), BF16),
        compiler_params=_params(("arbitrary", "arbitrary")),
        name="norm_mod",
    )(x, g, mod, mod)


def _proj_kernel(h_hbm, *refs, n_w, tw, epilogue, init):
    cast_w = refs[0].dtype != BF16
    n_scratch = 3 if cast_w else 2
    w_refs, rest = refs[:n_w], refs[n_w:-n_scratch]
    h_ref, sems = refs[-2:]
    wbf_ref = refs[-3] if cast_w else w_refs[0]
    tm, half = h_ref.shape[0], h_ref.shape[0] // 2
    i, j = pl.program_id(0), pl.program_id(1)
    n_i, n_j = pl.num_programs(0), pl.num_programs(1)

    def h_copy(tile, part):
        src_rows = pl.ds(pl.multiple_of(tile * tm + part * half, half), half)
        return pltpu.make_async_copy(h_hbm.at[src_rows, :],
                                     h_ref.at[part * half:(part + 1) * half, :], sems.at[part])

    def body(wait_current, prefetch_next):
        if init is not None:
            init(*rest)
        if cast_w:
            for s, w_ref in enumerate(w_refs):
                wbf_ref[:, s * tw:(s + 1) * tw] = w_ref[...].astype(BF16)
        carry, r0 = None, 0
        for size in _row_splits(tm):
            for part in range(2):
                if wait_current and r0 == part * half:
                    h_copy(i, part).wait()
            rows = slice(r0, r0 + size)
            acc = jnp.dot(h_ref[rows, :wbf_ref.shape[0]], wbf_ref[...],
                          preferred_element_type=F32)
            r0 += size
            for part in range(2):
                if prefetch_next and r0 == (part + 1) * half:
                    h_copy(i + 1, part).start()
            carry = epilogue(acc, rows, carry, *rest)

    @pl.when(jnp.logical_and(i == 0, j == 0))
    def _():
        h_copy(0, 0).start()
        h_copy(0, 1).start()

    is_first = j == 0
    is_last = jnp.logical_and(j == n_j - 1, i < n_i - 1)
    pl.when(is_first)(lambda: body(True, False))
    pl.when(is_last)(lambda: body(False, True))
    pl.when(jnp.logical_not(jnp.logical_or(is_first, is_last)))(lambda: body(False, False))


def _row_splits(tm):
    return [ROW_SUB] * (tm // ROW_SUB - 1) + [ROW_SUB // 2] * 2


def _proj(h, w, w_offsets, tw, n_tiles, epilogue, extra, extra_specs, out_shape, out_specs, name,
          init=None):
    r, tm, kp = h.shape
    k = w.shape[0]
    n_w = len(w_offsets)
    splits = _row_splits(tm)
    assert n_tiles >= 2 and sum(splits) == tm and tm // 2 in itertools.accumulate(splits)
    w_specs = [pl.BlockSpec((k, tw), functools.partial(lambda i, j, off: (0, off + j), off=off))
               for off in w_offsets]
    return pl.pallas_call(
        functools.partial(_proj_kernel, n_w=n_w, tw=tw, epilogue=epilogue, init=init),
        grid=(r, n_tiles),
        in_specs=[pl.BlockSpec(memory_space=pl.ANY)] + w_specs + list(extra_specs),
        out_specs=out_specs,
        out_shape=out_shape,
        scratch_shapes=([pltpu.VMEM((k, n_w * tw), BF16)] if w.dtype != BF16 else [])
        + [pltpu.VMEM((tm, kp), BF16), pltpu.SemaphoreType.DMA((2,))],
        compiler_params=_params(("arbitrary", "arbitrary")),
        name=name,
    )(h.reshape(r * tm, kp), *([w] * n_w), *extra)


def _ug_epilogue(acc, rows, _, c_ref, wa_ref, ba_ref, o_ref, mod_ref):
    tw = acc.shape[1] // 2
    o_ref[0, rows, :] = (_gelu(acc[:, :tw]) * jax.nn.sigmoid(acc[:, tw:])).astype(o_ref.dtype)


def _ug_init(c_ref, wa_ref, ba_ref, o_ref, mod_ref):
    _ada_kernel(c_ref, wa_ref, ba_ref, mod_ref)


def _v_epilogue(acc, rows, _, o_ref, ssq_ref):
    g = _gelu(acc)
    o_ref[0, rows, :] = g.astype(o_ref.dtype)
    ssq_ref[0, rows, :] += _lane_group_sum(g * g)


def _v_init(o_ref, ssq_ref):
    @pl.when(pl.program_id(1) == 0)
    def _():
        ssq_ref[...] = jnp.zeros_like(ssq_ref)


def _qk_epilogue(acc, rows, _, g_ref, o_ref):
    for k in range(acc.shape[1] // LANES):
        cols = slice(k * LANES, (k + 1) * LANES)
        z = acc[:, cols]
        ms = jnp.mean(z * z, axis=-1, keepdims=True)
        o_ref[0, rows, cols] = (z * lax.rsqrt(ms + EPS) * g_ref[:, cols]).astype(BF16)


def _transpose_epilogue(acc, rows, _, o_ref):
    o_ref[0, :, rows] = acc.T.astype(BF16)


def _sigmoid_epilogue(acc, rows, _, o_ref):
    o_ref[0, rows, :] = jax.nn.sigmoid(acc).astype(o_ref.dtype)


def _residual_epilogue(acc, rows, _, x_ref, gate_ref, o_ref):
    o_ref[0, rows, :] = x_ref[0, rows, :] + gate_ref[0] * acc


def _lane_group_sum(v):
    part = v[:, :LANES]
    for k in range(1, v.shape[1] // LANES):
        part = part + v[:, k * LANES:(k + 1) * LANES]
    return part


def _cast_w_down(cwg_ref, cwv_ref, cbg_ref, cbv_ref, wd_ref, o_ref, wd_bf16_ref):
    wd_bf16_ref[...] = wd_ref[...].astype(BF16)


def _conv_swiglu_epilogue(acc, rows, prev_tail, cwg_ref, cwv_ref, cbg_ref, cbv_ref, wd_ref, o_ref,
                          wd_bf16_ref):
    tw = acc.shape[1] // 2
    if prev_tail is None:
        prev_tail = jnp.zeros((SUBLANES, acc.shape[1]), F32)
    row = lax.broadcasted_iota(jnp.int32, (SUBLANES, acc.shape[1]), 0)

    def shifted(k):
        rolled = pltpu.roll(acc, k, 0)
        head = jnp.where(row >= k, rolled[:SUBLANES], pltpu.roll(prev_tail, k, 0))
        return jnp.concatenate([head, rolled[SUBLANES:]], axis=0)

    a1, a2 = shifted(1), shifted(2)

    def conv(lo, cw_ref, cb_ref):
        sl = slice(lo, lo + tw)
        return (cw_ref[0:1, :] * a2[:, sl] + cw_ref[1:2, :] * a1[:, sl]
                + cw_ref[2:3, :] * acc[:, sl] + cb_ref[...])

    gate = conv(0, cwg_ref, cbg_ref)
    val = conv(tw, cwv_ref, cbv_ref)
    o_ref[0, rows, :] = (gate * jax.nn.sigmoid(gate) * val).astype(BF16)
    return acc[-SUBLANES:]


def _spatial_gating(rows, ws, ug_ref, gv_ref, ssq_ref, sgug_ref, bs_ref):
    d_total = gv_ref.shape[2] * SGU_GROUPS
    ms = jnp.sum(ssq_ref[0, rows, :], axis=-1, keepdims=True) * (1.0 / d_total)
    vn = (gv_ref[0, rows, :] * lax.rsqrt(ms + EPS) * sgug_ref[...]).astype(BF16)
    mixed = jnp.dot(ws, vn, preferred_element_type=F32) + bs_ref[0]
    return ug_ref[0, rows, :] * mixed


def _mixer_kernel(q_ref, k_ref, vt_ref, sgb_ref, ug_ref, gv_ref, ssq_ref, sgug_ref, ws_ref, bs_ref,
                  lam_ref, subg_ref, o_ref, *, tq, hd, lambda_init):
    s_len = q_ref.shape[1]
    ws_row = lax.broadcasted_iota(jnp.int32, (CHUNK, CHUNK), 0)
    ws_col = lax.broadcasted_iota(jnp.int32, (CHUNK, CHUNK), 1)
    ws = jnp.where(ws_row >= ws_col, ws_ref[0], 0.0).astype(BF16)
    head = pl.program_id(1)
    log2e = math.log2(math.e)
    slope2 = log2e * jnp.exp2(jnp.full((1, 1), -8.0 / DIFF_HEADS, F32) * (head + 1).astype(F32))
    kv_i = lax.broadcasted_iota(jnp.int32, (tq, tq), 0)
    q_i = lax.broadcasted_iota(jnp.int32, (tq, tq), 1)
    rel_bias = slope2 * (q_i - kv_i).astype(F32)
    causal = q_i >= kv_i

    lv = lam_ref[...]
    lam = (jnp.exp(jnp.sum(lv[0:1] * lv[1:2], axis=-1, keepdims=True))
           - jnp.exp(jnp.sum(lv[2:3] * lv[3:4], axis=-1, keepdims=True)) + lambda_init)

    for qi in range(s_len // tq):
        kv_len = (qi + 1) * tq
        q_rows = slice(qi * tq, (qi + 1) * tq)
        chunk_off = [slope2 * float(tq * (qi - r)) for r in range(qi + 1)]
        probs, norms = [], []
        for c in range(2):
            qc = q_ref[0, q_rows, c * hd:(c + 1) * hd]
            kc = k_ref[0, :kv_len, c * hd:(c + 1) * hd]
            s_all = lax.dot_general(kc, qc, (((1,), (1,)), ((), ())),
                                    preferred_element_type=F32)
            t, m = [], None
            for r in range(qi + 1):
                tr = s_all[r * tq:(r + 1) * tq] - rel_bias
                if r == qi:
                    tr = jnp.where(causal, tr, NEG_INF)
                mr = jnp.max(tr, axis=0, keepdims=True) - chunk_off[r]
                m = mr if m is None else jnp.maximum(m, mr)
                t.append(tr)
            p, l = [], None
            for r in range(qi + 1):
                pr = jnp.exp2(t[r] - (m + chunk_off[r]))
                lr = jnp.sum(pr, axis=0, keepdims=True)
                l = lr if l is None else l + lr
                p.append(pr)
            probs.append(p)
            norms.append(l)
        w0 = 1.0 / norms[0]
        w1 = lam / norms[1]
        a = jnp.concatenate([(probs[0][r] * w0 - probs[1][r] * w1).astype(BF16)
                             for r in range(qi + 1)], axis=0)
        o = jnp.dot(vt_ref[0, :, :kv_len], a, preferred_element_type=F32).T
        ms = jnp.mean(o * o, axis=-1, keepdims=True)
        yb = o * lax.rsqrt(ms + EPS) * subg_ref[...] * (1.0 - lambda_init)
        ya = jnp.concatenate(
            [_spatial_gating(slice(r0, r0 + CHUNK), ws, ug_ref, gv_ref, ssq_ref, sgug_ref, bs_ref)
             for r0 in range(qi * tq, (qi + 1) * tq, CHUNK)], axis=0)
        o_ref[0, q_rows, :] = (sgb_ref[0, q_rows, :] * yb + ya).astype(BF16)


def _mixer(qk, vt, sgb, ug, gv, ssq, sgu_g, w_spatial, b_spatial, lam_rows, subln_g, lambda_init,
           tq=256):
    b, d, s = vt.shape
    vd = d // DIFF_HEADS
    hd = vd // 2
    assert SGU_GROUPS == DIFF_HEADS and tq % CHUNK == 0
    head_spec = pl.BlockSpec((1, s, vd), lambda i, h: (i, 0, h))
    return pl.pallas_call(
        functools.partial(_mixer_kernel, tq=tq, hd=hd, lambda_init=lambda_init),
        grid=(b, DIFF_HEADS),
        in_specs=[head_spec,
                  pl.BlockSpec((1, s, vd), lambda i, h: (i, 0, DIFF_HEADS + h)),
                  pl.BlockSpec((1, vd, s), lambda i, h: (i, h, 0)),
                  head_spec, head_spec, head_spec,
                  pl.BlockSpec((1, s, LANES), lambda i, h: (i, 0, 0)),
                  pl.BlockSpec((1, vd), lambda i, h: (0, h)),
                  pl.BlockSpec((1, CHUNK, CHUNK), lambda i, h: (h, 0, 0)),
                  pl.BlockSpec((1, CHUNK, 1), lambda i, h: (h, 0, 0)),
                  pl.BlockSpec((4, hd), lambda i, h: (0, 0)),
                  pl.BlockSpec((1, vd), lambda i, h: (0, 0))],
        out_specs=head_spec,
        out_shape=jax.ShapeDtypeStruct((b, s, d), BF16),
        compiler_params=_params(("arbitrary", "arbitrary")),
        name="mixer",
    )(qk, qk, vt, sgb, ug, gv, ssq, sgu_g, w_spatial, b_spatial[:, :, None], lam_rows, subln_g)


MXU_TILE = 256
PAIR_TW = MXU_TILE
WIDE_TW = 2 * MXU_TILE


def _layer(x, c_pad, lambda_init, w_ada, b_ada, norm1_g, norm2_g, w_in, sgu_norm_g, w_spatial,
           b_spatial, q_norm_g, k_norm_g, lam_rows, subln_g, w_out, w_ff_up, conv_w, conv_b,
           w_ff_down):
    b, s, d = x.shape
    f = w_ff_down.shape[0]
    mod1 = _ada(c_pad, w_ada, b_ada, 2 * d)[:b].reshape(b * 2, 1, d)

    def out3(dtype, n=d):
        return jax.ShapeDtypeStruct((b, s, n), dtype)

    def tile_spec(width):
        return pl.BlockSpec((1, s, width), lambda i, j: (i, 0, j))

    h = _norm_mod(x, norm1_g, mod1, 2, scale_idx=1, shift_idx=0)

    pair_seg, wide_seg = d // PAIR_TW, d // WIDE_TW
    assert b * pair_seg * PAIR_TW == 4 * d
    ada_rows = c_pad.shape[0]
    ug, mod2 = _proj(
        h, w_in, (0 * pair_seg, 5 * pair_seg), PAIR_TW, pair_seg, _ug_epilogue,
        (c_pad, w_ada, b_ada),
        (pl.BlockSpec((ada_rows, d), lambda i, j: (0, 0)),
         pl.BlockSpec((d, PAIR_TW), lambda i, j: (0, 2 * pair_seg + i * pair_seg + j)),
         pl.BlockSpec((1, PAIR_TW), lambda i, j: (0, 2 * pair_seg + i * pair_seg + j))),
        (out3(BF16), jax.ShapeDtypeStruct((ada_rows, 4 * d), F32)),
        (tile_spec(PAIR_TW), pl.BlockSpec((ada_rows, PAIR_TW), lambda i, j: (0, i * pair_seg + j))),
        "proj_ug", init=_ug_init)
    mod2 = mod2[:b].reshape(b * 4, 1, d)
    gv, ssq = _proj(h, w_in, (1 * wide_seg,), WIDE_TW, wide_seg, _v_epilogue, (), (),
                    (out3(BF16), out3(F32, LANES)),
                    (tile_spec(WIDE_TW), pl.BlockSpec((1, s, LANES), lambda i, j: (i, 0, 0))),
                    "proj_v", init=_v_init)
    hd = q_norm_g.shape[-1]
    q_gain = q_norm_g * (hd ** -0.5 * math.log2(math.e))
    qk_g = jnp.concatenate([jnp.tile(q_gain, d // hd), jnp.tile(k_norm_g, d // hd)])[None]
    qk = _proj(h, w_in, (2 * wide_seg,), WIDE_TW, 2 * wide_seg, _qk_epilogue, (qk_g,),
               (pl.BlockSpec((1, WIDE_TW), lambda i, j: (0, j)),),
               out3(BF16, 2 * d), tile_spec(WIDE_TW), "proj_qk")
    vbt = _proj(h, w_in, (4 * wide_seg,), WIDE_TW, wide_seg, _transpose_epilogue, (), (),
                jax.ShapeDtypeStruct((b, d, s), BF16),
                pl.BlockSpec((1, WIDE_TW, s), lambda i, j: (i, j, 0)), "proj_vb")
    sgb = _proj(h, w_in, (6 * wide_seg,), WIDE_TW, wide_seg, _sigmoid_epilogue, (), (),
                out3(BF16), tile_spec(WIDE_TW), "proj_gb")

    y = _mixer(qk, vbt, sgb, ug, gv, ssq, sgu_norm_g[None], w_spatial, b_spatial, lam_rows,
               subln_g[None], lambda_init)

    tm = s
    rows3 = (b * s // tm, tm, d)
    x1 = _proj(y.reshape(rows3), w_out, (0,), WIDE_TW, d // WIDE_TW, _residual_epilogue,
               (x.reshape(rows3), mod2),
               (pl.BlockSpec((1, tm, WIDE_TW), lambda i, j: (i, 0, j)),
                pl.BlockSpec((1, 1, WIDE_TW), lambda i, j: ((i // (s // tm)) * 4 + 0, 0, j))),
               jax.ShapeDtypeStruct(rows3, F32),
               pl.BlockSpec((1, tm, WIDE_TW), lambda i, j: (i, 0, j)), "proj_out")
    x1 = x1.reshape(b, s, d)

    h2 = _norm_mod(x1, norm2_g, mod2, 4, scale_idx=2, shift_idx=1)
    val_off = f // PAIR_TW
    conv_b2 = conv_b[None]
    wd_rows = f // (b * val_off)
    assert wd_rows * b * val_off == f and wd_rows % (2 * SUBLANES) == 0
    wd_spec = pl.BlockSpec((wd_rows, d), lambda i, j: (i * val_off + j, 0))
    g, w_down_bf16 = _proj(
        h2, w_ff_up, (0, val_off), PAIR_TW, val_off, _conv_swiglu_epilogue,
        (conv_w, conv_w, conv_b2, conv_b2, w_ff_down),
        (pl.BlockSpec((CONV_WIDTH, PAIR_TW), lambda i, j: (0, j)),
         pl.BlockSpec((CONV_WIDTH, PAIR_TW), lambda i, j: (0, val_off + j)),
         pl.BlockSpec((1, PAIR_TW), lambda i, j: (0, j)),
         pl.BlockSpec((1, PAIR_TW), lambda i, j: (0, val_off + j)),
         wd_spec),
        (out3(BF16, f), jax.ShapeDtypeStruct((f, d), BF16)), (tile_spec(PAIR_TW), wd_spec),
        "proj_up", init=_cast_w_down)

    tm = s // 2
    rows3 = (b * s // tm, tm, d)
    out = _proj(g.reshape(b * s // tm, tm, f), w_down_bf16, (0,), WIDE_TW, d // WIDE_TW,
                _residual_epilogue, (x1.reshape(rows3), mod2),
                (pl.BlockSpec((1, tm, WIDE_TW), lambda i, j: (i, 0, j)),
                 pl.BlockSpec((1, 1, WIDE_TW), lambda i, j: ((i // (s // tm)) * 4 + 3, 0, j))),
                jax.ShapeDtypeStruct(rows3, F32),
                pl.BlockSpec((1, tm, WIDE_TW), lambda i, j: (i, 0, j)), "proj_down")
    return out.reshape(b, s, d)


def kernel(x, c, w_ada, b_ada, norm1_g, norm2_g, w_in, sgu_norm_g, w_spatial, b_spatial, q_norm_g, k_norm_g, lambda_q1, lambda_k1, lambda_q2, lambda_k2, subln_g, w_out, w_ff_up, conv_w, conv_b, w_ff_down):
    depth = w_in.shape[0]
    c_pad = jnp.pad(c, ((0, ADA_ROWS - c.shape[0]), (0, 0)))
    for l in range(depth):
        lam_rows = jnp.stack([lambda_q1[l], lambda_k1[l], lambda_q2[l], lambda_k2[l]])
        x = _layer(x, c_pad, 0.8 - 0.6 * math.exp(-0.3 * l), w_ada[l], b_ada[l][None],
                   norm1_g[l][None], norm2_g[l][None], w_in[l], sgu_norm_g[l], w_spatial[l],
                   b_spatial[l], q_norm_g[l], k_norm_g[l], lam_rows, subln_g[l], w_out[l],
                   w_ff_up[l], conv_w[l], conv_b[l], w_ff_down[l])
    return x
```
